```python
import math
import jax, jax.numpy as jnp
from jax import lax
import numpy as np

D_MODEL = 1024
BATCH = 8
SEQ = 2048
DEPTH = 1
DEC_BATCH = 16
DEC_SEQ = 64
PAST_LEN = 2048

CHUNK = 64
Q_BLOCK = 128
HEAD_DIM = 64
DIFF_HEADS = 4
SB_HEADS = 8
DIFF_WIDTH = DIFF_HEADS * 2 * HEAD_DIM
SB_WIDTH = SB_HEADS * HEAD_DIM
MIX_WIDTH = DIFF_WIDTH + SB_WIDTH
IN_WIDTH = 3 * MIX_WIDTH
D_FF = 4 * D_MODEL
EPS = 1e-6
NEG = -1e30

kernel_name = 'hymba_diff_stickbreaking_streaming_step'


def rmsnorm(x, g):
    xf = x.astype(jnp.float32)
    y = xf * lax.rsqrt(jnp.mean(xf * xf, axis=-1, keepdims=True) + EPS)
    return (y * g.astype(jnp.float32)).astype(x.dtype)


def alibi_slopes():
    return jnp.asarray(2.0 ** (-8.0 * np.arange(1, DIFF_HEADS + 1) / DIFF_HEADS), dtype=jnp.float32)


def lambda_init(layer):
    return 0.8 - 0.6 * math.exp(-0.3 * layer)


def modulate(c, w_ada, b_ada):
    m = jax.nn.silu(c) @ w_ada + b_ada
    return jnp.split(m[:, None, :], 6, axis=-1)


def split_proj(u):
    b, t, _ = u.shape
    cuts = [DIFF_WIDTH, 2 * DIFF_WIDTH, 3 * DIFF_WIDTH, 3 * DIFF_WIDTH + SB_WIDTH, 3 * DIFF_WIDTH + 2 * SB_WIDTH]
    qd, kd, vd, qs, ks, vs = jnp.split(u, cuts, axis=-1)
    qd = qd.reshape(b, t, DIFF_HEADS, 2, HEAD_DIM)
    kd = kd.reshape(b, t, DIFF_HEADS, 2 * HEAD_DIM)
    vd = vd.reshape(b, t, DIFF_HEADS, 2 * HEAD_DIM)
    qs = qs.reshape(b, t, SB_HEADS, HEAD_DIM)
    ks = ks.reshape(b, t, SB_HEADS, HEAD_DIM)
    vs = vs.reshape(b, t, SB_HEADS, HEAD_DIM)
    return qd, kd, vd, qs, ks, vs


def mix_queries(qd, qs, q_pos, kd, vd, ks, vs, k_pos, lam, subln_g, lam_init):
    f32 = jnp.float32
    b, tk = kd.shape[0], kd.shape[1]
    tq = qd.shape[1]
    scale = HEAD_DIM ** -0.5
    kd2 = kd.reshape(b, tk, DIFF_HEADS, 2, HEAD_DIM)
    s = jnp.einsum('bqhmd,bkhmd->bhmqk', qd.astype(f32), kd2.astype(f32)) * scale
    dist = jnp.abs(q_pos[:, None] - k_pos[None, :]).astype(f32)
    s = s - alibi_slopes()[None, :, None, None, None] * dist
    visible = (k_pos[None, :] // CHUNK) <= (q_pos[:, None] // CHUNK)
    p = jax.nn.softmax(jnp.where(visible, s, NEG), axis=-1)
    a = p[:, :, 0] - lam * p[:, :, 1]
    od = jnp.einsum('bhqk,bkhe->bqhe', a, vd.astype(f32))
    od = rmsnorm(od, subln_g) * (1.0 - lam_init)
    z = jnp.einsum('bqhd,bkhd->bhqk', qs.astype(f32), ks.astype(f32)) * scale
    earlier = k_pos[None, :] < q_pos[:, None]
    log_keep = jnp.where(earlier, jax.nn.log_sigmoid(-z), 0.0)
    between = lax.cumsum(log_keep, axis=3, reverse=True) - log_keep
    w = jnp.where(earlier, jnp.exp(jax.nn.log_sigmoid(z) + between), 0.0)
    osb = jnp.einsum('bhqk,bkhd->bqhd', w, vs.astype(f32))
    return jnp.concatenate([od.reshape(b, tq, DIFF_WIDTH), osb.reshape(b, tq, SB_WIDTH)], axis=-1)


def prompt_mix(qd, kd, vd, qs, ks, vs, lam, subln_g, lam_init):
    b, s = qd.shape[0], qd.shape[1]
    nblk = s // Q_BLOCK
    k_pos = jnp.arange(s)
    qd_blocks = qd.reshape(b, nblk, Q_BLOCK, DIFF_HEADS, 2, HEAD_DIM).swapaxes(0, 1)
    qs_blocks = qs.reshape(b, nblk, Q_BLOCK, SB_HEADS, HEAD_DIM).swapaxes(0, 1)

    def blk(args):
        i, qd_b, qs_b = args
        q_pos = i * Q_BLOCK + jnp.arange(Q_BLOCK)
        return mix_queries(qd_b, qs_b, q_pos, kd, vd, ks, vs, k_pos, lam, subln_g, lam_init)

    out = lax.map(blk, (jnp.arange(nblk), qd_blocks, qs_blocks))
    return out.swapaxes(0, 1).reshape(b, s, MIX_WIDTH)


def sample_mix(qd, kd, vd, qs, ks, vs, ck_d, cv_d, ck_s, cv_s, lam, subln_g, lam_init):
    past, t = ck_d.shape[1], qd.shape[1]
    q_pos = past + jnp.arange(t)
    k_pos = jnp.arange(past + t)
    kd_all = jnp.concatenate([ck_d.astype(kd.dtype), kd], axis=1)
    vd_all = jnp.concatenate([cv_d.astype(vd.dtype), vd], axis=1)
    ks_all = jnp.concatenate([ck_s.astype(ks.dtype), ks], axis=1)
    vs_all = jnp.concatenate([cv_s.astype(vs.dtype), vs], axis=1)
    return mix_queries(qd, qs, q_pos, kd_all, vd_all, ks_all, vs_all, k_pos, lam, subln_g, lam_init)


def pre(x, g, shift, scale):
    return rmsnorm(x, g) * (1.0 + scale) + shift


def post_add(x, y, g, gate):
    return x + gate * rmsnorm(y, g)


def ffn(h, w_up, w_down):
    return jnp.square(jax.nn.relu(h @ w_up)) @ w_down


def setup_inputs(seed: int = 0) -> dict:
    key = jax.random.key(seed)
    ks = jax.random.split(key, 24)
    f32 = jnp.float32
    nrm = lambda k, shape, s: jax.random.normal(k, shape, f32) * s
    gain = lambda k, shape: 1.0 + 0.1 * jax.random.normal(k, shape, f32)
    return {
        'x_prompt': nrm(ks[0], (BATCH, SEQ, D_MODEL), 1.0),
        'x_sample': nrm(ks[1], (DEC_BATCH, DEC_SEQ, D_MODEL), 1.0),
        'c_prompt': nrm(ks[2], (BATCH, D_MODEL), 1.0),
        'c_sample': nrm(ks[3], (DEC_BATCH, D_MODEL), 1.0),
        'cache_diff_k': nrm(ks[4], (DEPTH, DEC_BATCH, PAST_LEN, DIFF_HEADS, 2 * HEAD_DIM), 1.0),
        'cache_diff_v': nrm(ks[5], (DEPTH, DEC_BATCH, PAST_LEN, DIFF_HEADS, 2 * HEAD_DIM), 1.0),
        'cache_sb_k': nrm(ks[6], (DEPTH, DEC_BATCH, PAST_LEN, SB_HEADS, HEAD_DIM), 1.0),
        'cache_sb_v': nrm(ks[7], (DEPTH, DEC_BATCH, PAST_LEN, SB_HEADS, HEAD_DIM), 1.0),
        'w_ada': nrm(ks[8], (DEPTH, D_MODEL, 6 * D_MODEL), 0.2 * D_MODEL ** -0.5),
        'b_ada': nrm(ks[9], (DEPTH, 6 * D_MODEL), 0.02),
        'g_pre_mix': gain(ks[10], (DEPTH, D_MODEL)),
        'g_post_mix': gain(ks[11], (DEPTH, D_MODEL)),
        'w_in': nrm(ks[12], (DEPTH, D_MODEL, IN_WIDTH), D_MODEL ** -0.5),
        'lambda_q1': nrm(ks[13], (DEPTH, HEAD_DIM), 0.1),
        'lambda_k1': nrm(ks[14], (DEPTH, HEAD_DIM), 0.1),
        'lambda_q2': nrm(ks[15], (DEPTH, HEAD_DIM), 0.1),
        'lambda_k2': nrm(ks[16], (DEPTH, HEAD_DIM), 0.1),
        'diff_subln_g': gain(ks[17], (DEPTH, 2 * HEAD_DIM)),
        'w_out': nrm(ks[18], (DEPTH, MIX_WIDTH, D_MODEL), MIX_WIDTH ** -0.5),
        'g_pre_ffn': gain(ks[19], (DEPTH, D_MODEL)),
        'g_post_ffn': gain(ks[20], (DEPTH, D_MODEL)),
        'w_up': nrm(ks[21], (DEPTH, D_MODEL, D_FF), D_MODEL ** -0.5),
        'w_down': nrm(ks[22], (DEPTH, D_FF, D_MODEL), D_FF ** -0.5),
    }


def reference(x_prompt, x_sample, c_prompt, c_sample, cache_diff_k, cache_diff_v, cache_sb_k, cache_sb_v,
              w_ada, b_ada, g_pre_mix, g_post_mix, w_in, lambda_q1, lambda_k1, lambda_q2, lambda_k2,
              diff_subln_g, w_out, g_pre_ffn, g_post_ffn, w_up, w_down):
    hp, hs = x_prompt, x_sample
    dkp, dvp, skp, svp, dks, dvs, sks, svs = [], [], [], [], [], [], [], []
    for l in range(DEPTH):
        lam_init = lambda_init(l)
        f32 = jnp.float32
        lam = (jnp.exp(jnp.sum(lambda_q1[l].astype(f32) * lambda_k1[l].astype(f32)))
               - jnp.exp(jnp.sum(lambda_q2[l].astype(f32) * lambda_k2[l].astype(f32))) + lam_init)
        sh1p, sc1p, ga1p, sh2p, sc2p, ga2p = modulate(c_prompt, w_ada[l], b_ada[l])
        sh1s, sc1s, ga1s, sh2s, sc2s, ga2s = modulate(c_sample, w_ada[l], b_ada[l])
        qd, kd, vd, qs, ks_, vs = split_proj(pre(hp, g_pre_mix[l], sh1p, sc1p) @ w_in[l])
        mix_p = prompt_mix(qd, kd, vd, qs, ks_, vs, lam, diff_subln_g[l], lam_init).astype(hp.dtype)
        hp = post_add(hp, mix_p @ w_out[l], g_post_mix[l], ga1p)
        hp = post_add(hp, ffn(pre(hp, g_pre_ffn[l], sh2p, sc2p), w_up[l], w_down[l]), g_post_ffn[l], ga2p)
        dkp.append(kd); dvp.append(vd); skp.append(ks_); svp.append(vs)
        qd, kd, vd, qs, ks_, vs = split_proj(pre(hs, g_pre_mix[l], sh1s, sc1s) @ w_in[l])
        mix_s = sample_mix(qd, kd, vd, qs, ks_, vs, cache_diff_k[l], cache_diff_v[l], cache_sb_k[l], cache_sb_v[l],
                           lam, diff_subln_g[l], lam_init).astype(hs.dtype)
        hs = post_add(hs, mix_s @ w_out[l], g_post_mix[l], ga1s)
        hs = post_add(hs, ffn(pre(hs, g_pre_ffn[l], sh2s, sc2s), w_up[l], w_down[l]), g_post_ffn[l], ga2s)
        dks.append(kd); dvs.append(vd); sks.append(ks_); svs.append(vs)
    diff_k_prompt = jnp.stack(dkp)
    diff_v_prompt = jnp.stack(dvp)
    sb_k_prompt = jnp.stack(skp)
    sb_v_prompt = jnp.stack(svp)
    diff_k_sample = jnp.stack(dks)
    diff_v_sample = jnp.stack(dvs)
    sb_k_sample = jnp.stack(sks)
    sb_v_sample = jnp.stack(svs)
    return (hp, hs, diff_k_prompt, diff_v_prompt, sb_k_prompt, sb_v_prompt,
            diff_k_sample, diff_v_sample, sb_k_sample, sb_v_sample)
```

```python
import functools
import math

import jax
import jax.numpy as jnp
import numpy as np
from jax import lax
from jax.experimental import pallas as pl
from jax.experimental.pallas import tpu as pltpu

D_MODEL = 1024
CHUNK = 64
HEAD_DIM = 64
DIFF_HEADS = 4
SB_HEADS = 8
SLAB = 128
DIFF_WIDTH = DIFF_HEADS * 2 * HEAD_DIM
SB_WIDTH = SB_HEADS * HEAD_DIM
MIX_WIDTH = DIFF_WIDTH + SB_WIDTH
IN_WIDTH = 3 * MIX_WIDTH
D_FF = 4 * D_MODEL
EPS = 1e-6
NEG = -1e30
LAMBDA_INIT = 0.8 - 0.6 * math.exp(-0.3 * 0)
ATTN_SCALE = HEAD_DIM ** -0.5
ALIBI_SLOPES = tuple(float(2.0 ** (-8.0 * (i + 1) / DIFF_HEADS)) for i in range(DIFF_HEADS))

VMEM_LIMIT = 56 * 1024 * 1024
BF16 = jnp.bfloat16
F32 = jnp.float32

NT_DIMS = (((1,), (1,)), ((), ()))
TN_DIMS = (((0,), (0,)), ((), ()))


def _rms(x, g):
    ms = jnp.mean(x * x, axis=-1, keepdims=True)
    return x * lax.rsqrt(ms + EPS) * g


def _mod_kernel(c_ref, w_ref, b_ref, lq1_ref, lk1_ref, lq2_ref, lk2_ref, m_ref, lam_ref):
    c = c_ref[...]
    s = c * jax.nn.sigmoid(c)
    m_ref[...] = jnp.dot(s, w_ref[...], preferred_element_type=F32,
                         precision=lax.Precision.HIGHEST) + b_ref[...]
    d1 = jnp.sum(lq1_ref[...] * lk1_ref[...], axis=-1, keepdims=True)
    d2 = jnp.sum(lq2_ref[...] * lk2_ref[...], axis=-1, keepdims=True)
    lam = jnp.exp(d1) - jnp.exp(d2) + LAMBDA_INIT
    lam_ref[...] = jnp.broadcast_to(lam, lam_ref.shape)


def _modulation(c_all, w_ada, b_ada, lq1, lk1, lq2, lk2):
    nb = c_all.shape[0]
    tn = 1024
    vec = pl.BlockSpec((1, HEAD_DIM), lambda j: (0, 0))
    return pl.pallas_call(
        _mod_kernel,
        grid=(6 * D_MODEL // tn,),
        in_specs=[pl.BlockSpec((nb, D_MODEL), lambda j: (0, 0)),
                  pl.BlockSpec((D_MODEL, tn), lambda j: (0, j)),
                  pl.BlockSpec((1, tn), lambda j: (0, j)),
                  vec, vec, vec, vec],
        out_specs=[pl.BlockSpec((nb, tn), lambda j: (0, j)),
                   pl.BlockSpec((8, 128), lambda j: (0, 0))],
        out_shape=[jax.ShapeDtypeStruct((nb, 6 * D_MODEL), F32),
                   jax.ShapeDtypeStruct((8, 128), F32)],
        compiler_params=pltpu.CompilerParams(dimension_semantics=("arbitrary",),
                                             vmem_limit_bytes=VMEM_LIMIT),
        name="modulation",
    )(c_all, w_ada, b_ada, lq1, lk1, lq2, lk2)


def _in_kernel(x_ref, mod_ref, g_ref, w_ref,
               q_ref, kb_ref, vb_ref, kd_ref, vd_ref, ks_ref, vs_ref):
    bb, tm, d = x_ref.shape
    x = x_ref[...]
    shift = mod_ref[:, 0:1, :]
    scale = mod_ref[:, 1:2, :]
    h = _rms(x, g_ref[...]) * (1.0 + scale) + shift
    hb = h.reshape(bb * tm, d).astype(BF16)

    def proj(c):
        u = jnp.dot(hb, w_ref[:, c * 512:(c + 1) * 512], preferred_element_type=F32)
        return u.reshape(bb, tm, 512)

    q_ref[:, :, 0:512] = (proj(0) * ATTN_SCALE).astype(BF16)
    q_ref[:, :, 512:1024] = (proj(3) * ATTN_SCALE).astype(BF16)
    for c, f_ref, b_ref, lo in ((1, kd_ref, kb_ref, 0), (2, vd_ref, vb_ref, 0),
                                (4, ks_ref, kb_ref, 512), (5, vs_ref, vb_ref, 512)):
        u = proj(c)
        f_ref[...] = u
        b_ref[:, :, lo:lo + 512] = u.astype(BF16)


def _in_proj(x, mods, mod_off, g, w_bf, bb, tm):
    b, s, d = x.shape
    grid = (b // bb, s // tm)
    tok = lambda w: pl.BlockSpec((bb, tm, w), lambda i, t: (i, t, 0))
    const = lambda shape: pl.BlockSpec(shape, lambda i, t: (0,) * len(shape),
                                       pipeline_mode=pl.Buffered(1))
    outs = [jax.ShapeDtypeStruct((b, s, MIX_WIDTH), BF16)] * 3 + \
           [jax.ShapeDtypeStruct((b, s, 512), F32)] * 4
    return pl.pallas_call(
        _in_kernel,
        grid=grid,
        in_specs=[tok(d),
                  pl.BlockSpec((bb, 6, d), lambda i, t: (i + mod_off // bb, 0, 0)),
                  const((1, d)),
                  const((d, IN_WIDTH))],
        out_specs=[tok(MIX_WIDTH)] * 3 + [tok(512)] * 4,
        out_shape=outs,
        compiler_params=pltpu.CompilerParams(dimension_semantics=("parallel", "parallel"),
                                             vmem_limit_bytes=VMEM_LIMIT),
        name="in_proj",
    )(x, mods, g, w_bf)


def _half_masks(q):
    lane = lax.broadcasted_iota(jnp.int32, q.shape, 1)
    zero = jnp.zeros_like(q)
    return jnp.where(lane < HEAD_DIM, q, zero), jnp.where(lane >= HEAD_DIM, q, zero)


def _positions(n, tq, k0, q0):
    kpos = k0 + lax.broadcasted_iota(jnp.int32, (n, tq), 0)
    qpos = q0 + lax.broadcasted_iota(jnp.int32, (n, tq), 1)
    return kpos, qpos


def _diff_kernel(slope_ref, lam_ref, q_ref, km_ref, vm_ref, kd_ref, vd_ref, g_ref, o_ref,
                 *, tk, n_main, q_base, diag_base):
    tq = q_ref.shape[1]
    td = kd_ref.shape[1]
    qi = pl.program_id(2)
    h = pl.program_id(1)
    slope = slope_ref[h]
    lam = lam_ref[0]
    q0 = q_base(qi)
    q1, q2 = _half_masks(q_ref[0])

    rel = (lax.broadcasted_iota(jnp.int32, (tk, tq), 1)
           - lax.broadcasted_iota(jnp.int32, (tk, tq), 0)).astype(F32)
    alibi_rel = -slope * rel

    def update(state, s, vblk):
        m, l, acc = state
        m_new = jnp.maximum(m, jnp.max(s, axis=0, keepdims=True))
        p = jnp.exp(s - m_new)
        alpha = jnp.exp(m - m_new)
        l_new = alpha * l + jnp.sum(p, axis=0, keepdims=True)
        pv = lax.dot_general(vblk, p.astype(BF16), TN_DIMS, preferred_element_type=F32)
        return m_new, l_new, alpha * acc + pv

    def main_body(j, carry):
        st1, st2 = carry
        k0 = pl.multiple_of(j * tk, tk)
        kblk = km_ref[0, pl.ds(k0, tk), :].astype(BF16)
        vblk = vm_ref[0, pl.ds(k0, tk), :].astype(BF16)
        bias = alibi_rel - slope * (q0 - k0).astype(F32)
        s1 = lax.dot_general(kblk, q1, NT_DIMS, preferred_element_type=F32) + bias
        s2 = lax.dot_general(kblk, q2, NT_DIMS, preferred_element_type=F32) + bias
        return update(st1, s1, vblk), update(st2, s2, vblk)

    def init():
        return (jnp.full((1, tq), NEG, F32), jnp.zeros((1, tq), F32), jnp.zeros((SLAB, tq), F32))

    st1, st2 = lax.fori_loop(0, n_main(qi), main_body, (init(), init()))

    kblk = kd_ref[0].astype(BF16)
    vblk = vd_ref[0].astype(BF16)
    kpos, qpos = _positions(td, tq, diag_base(qi), q0)
    visible = (kpos // CHUNK) <= (qpos // CHUNK)
    bias = -slope * jnp.abs(qpos - kpos).astype(F32)
    s1 = lax.dot_general(kblk, q1, NT_DIMS, preferred_element_type=F32) + bias
    s2 = lax.dot_general(kblk, q2, NT_DIMS, preferred_element_type=F32) + bias
    st1 = update(st1, jnp.where(visible, s1, NEG), vblk)
    st2 = update(st2, jnp.where(visible, s2, NEG), vblk)

    o = st1[2] / st1[1] - lam * (st2[2] / st2[1])
    ms = jnp.mean(o * o, axis=0, keepdims=True)
    o = o * lax.rsqrt(ms + EPS) * g_ref[...] * (1.0 - LAMBDA_INIT)
    o_ref[0] = o.T.astype(o_ref.dtype)


def _sb_kernel(q_ref, km_ref, vm_ref, kd_ref, vd_ref, um_ref, ud_ref, o_ref,
               *, tk, n_main, q_base, diag_base):
    tq = q_ref.shape[1]
    td = kd_ref.shape[1]
    qi = pl.program_id(2)
    q0 = q_base(qi)
    qa, qb = _half_masks(q_ref[0])

    def tile(state, z, vblk, u, earlier):
        carry, acc = state
        sp = jnp.log(1.0 + jnp.exp(-jnp.abs(z)))
        t = jnp.maximum(z, 0.0) + sp
        if earlier is not None:
            t = jnp.where(earlier, t, 0.0)
        t_hi = t.astype(BF16)
        t_lo = (t - t_hi.astype(F32)).astype(BF16)
        later = (jnp.dot(u, t_hi, preferred_element_type=F32)
                 + jnp.dot(u, t_lo, preferred_element_type=F32))
        w = jnp.exp(z - t - later - carry)
        if earlier is not None:
            w = jnp.where(earlier, w, 0.0)
        pv = lax.dot_general(vblk, w.astype(BF16), TN_DIMS, preferred_element_type=F32)
        return carry + jnp.sum(t, axis=0, keepdims=True), acc + pv

    def init():
        return (jnp.zeros((1, tq), F32), jnp.zeros((SLAB, tq), F32))

    kblk = kd_ref[0].astype(BF16)
    vblk = vd_ref[0].astype(BF16)
    kpos, qpos = _positions(td, tq, diag_base(qi), q0)
    earlier = kpos < qpos
    za = lax.dot_general(kblk, qa, NT_DIMS, preferred_element_type=F32)
    zb = lax.dot_general(kblk, qb, NT_DIMS, preferred_element_type=F32)
    sta = tile(init(), za, vblk, ud_ref[...], earlier)
    stb = tile(init(), zb, vblk, ud_ref[...], earlier)

    nm = n_main(qi)

    def main_body(i, carry):
        sta, stb = carry
        k0 = pl.multiple_of((nm - 1 - i) * tk, tk)
        kblk = km_ref[0, pl.ds(k0, tk), :].astype(BF16)
        vblk = vm_ref[0, pl.ds(k0, tk), :].astype(BF16)
        za = lax.dot_general(kblk, qa, NT_DIMS, preferred_element_type=F32)
        zb = lax.dot_general(kblk, qb, NT_DIMS, preferred_element_type=F32)
        u = um_ref[...]
        return tile(sta, za, vblk, u, None), tile(stb, zb, vblk, u, None)

    sta, stb = lax.fori_loop(0, nm, main_body, (sta, stb))
    row = lax.broadcasted_iota(jnp.int32, (SLAB, tq), 0)
    o = jnp.where(row < HEAD_DIM, sta[1], stb[1])
    o_ref[0] = o.T.astype(o_ref.dtype)


def _strict_upper(n):
    r = np.arange(n)
    return jnp.asarray((r[None, :] > r[:, None]).astype(np.float32), dtype=BF16)


def _attention(q, k_main, v_main, k_diag, v_diag, slopes, lam, subln_g, *,
               tq, tk, td, causal_blocks, past):
    b, t, _ = q.shape
    nq = t // tq
    grid = (b, DIFF_HEADS, nq)
    if causal_blocks:
        n_main = lambda qi: qi * (tq // tk)
        q_base = lambda qi: qi * tq
        diag_base = lambda qi: qi * tq
    else:
        n_main = lambda qi: past // tk
        q_base = lambda qi: past + qi * tq
        diag_base = lambda qi: past + qi * tq
    kw = dict(tk=tk, n_main=n_main, q_base=q_base, diag_base=diag_base)
    cparams = pltpu.CompilerParams(dimension_semantics=("parallel", "parallel", "arbitrary"),
                                   vmem_limit_bytes=VMEM_LIMIT)

    (kmd, kms), (vmd, vms) = k_main, v_main
    kdg, vdg = k_diag, v_diag
    tm_len = kmd.shape[1]

    def specs(slab_off_main, slab_off_new):
        q_spec = pl.BlockSpec((1, tq, SLAB), lambda i, h, j: (i, j, h + slab_off_new))
        main = pl.BlockSpec((1, tm_len, SLAB), lambda i, h, j: (i, 0, h + slab_off_main))
        diag = pl.BlockSpec((1, td, SLAB), lambda i, h, j: (i, j, h + slab_off_new))
        return q_spec, main, diag

    smem = pl.BlockSpec(memory_space=pltpu.SMEM)
    out_spec = pl.BlockSpec((1, tq, SLAB), lambda i, h, j: (i, j, h))
    out_shape = jax.ShapeDtypeStruct((b, t, DIFF_WIDTH), BF16)

    main_sb_off = 4 if kms is kmd else 0
    q_spec, main_spec, diag_spec = specs(0, 0)
    od = pl.pallas_call(
        functools.partial(_diff_kernel, **kw),
        grid=grid,
        in_specs=[smem, smem, q_spec, main_spec, main_spec, diag_spec, diag_spec,
                  pl.BlockSpec((SLAB, 1), lambda i, h, j: (0, 0))],
        out_specs=out_spec, out_shape=out_shape,
        compiler_params=cparams, name="diff_attention",
    )(slopes, lam, q, kmd, vmd, kdg, vdg, subln_g)

    q_spec, main_spec, diag_spec = specs(main_sb_off, 4)
    const2 = lambda n: pl.BlockSpec((n, n), lambda i, h, j: (0, 0))
    osb = pl.pallas_call(
        functools.partial(_sb_kernel, **kw),
        grid=grid,
        in_specs=[q_spec, main_spec, main_spec, diag_spec, diag_spec, const2(tk), const2(td)],
        out_specs=out_spec, out_shape=out_shape,
        compiler_params=cparams, name="sb_attention",
    )(q, kms, vms, kdg, vdg, _strict_upper(tk), _strict_upper(td))
    return od, osb


def _out_kernel(od_ref, osb_ref, x_ref, mod_ref, gpm_ref, gpf_ref, gqf_ref,
                wo_ref, wu_ref, wd_ref, y_ref):
    bb, tm, d = x_ref.shape
    n = bb * tm
    od = od_ref[...].reshape(n, DIFF_WIDTH)
    osb = osb_ref[...].reshape(n, SB_WIDTH)
    y = (jnp.dot(od, wo_ref[0:DIFF_WIDTH, :], preferred_element_type=F32)
         + jnp.dot(osb, wo_ref[DIFF_WIDTH:MIX_WIDTH, :], preferred_element_type=F32))
    y = y.reshape(bb, tm, d)
    gate1 = mod_ref[:, 2:3, :]
    shift2 = mod_ref[:, 3:4, :]
    scale2 = mod_ref[:, 4:5, :]
    gate2 = mod_ref[:, 5:6, :]
    x1 = x_ref[...] + gate1 * _rms(y, gpm_ref[...])
    h2 = _rms(x1, gpf_ref[...]) * (1.0 + scale2) + shift2
    h2b = h2.reshape(n, d).astype(BF16)
    acc = jnp.zeros((n, d), F32)
    fc = 1024
    for c in range(D_FF // fc):
        f = jnp.dot(h2b, wu_ref[:, c * fc:(c + 1) * fc], preferred_element_type=F32)
        r = jnp.square(jnp.maximum(f, 0.0)).astype(BF16)
        acc = acc + jnp.dot(r, wd_ref[c * fc:(c + 1) * fc, :], preferred_element_type=F32)
    y2 = acc.reshape(bb, tm, d)
    y_ref[...] = x1 + gate2 * _rms(y2, gqf_ref[...])


def _out_ffn(od, osb, x, mods, mod_off, g_post_mix, g_pre_ffn, g_post_ffn,
             wo_bf, wu_bf, wd_bf, bb, tm):
    b, s, d = x.shape
    grid = (b // bb, s // tm)
    tok = lambda w: pl.BlockSpec((bb, tm, w), lambda i, t: (i, t, 0))
    const = lambda shape: pl.BlockSpec(shape, lambda i, t: (0,) * len(shape),
                                       pipeline_mode=pl.Buffered(1))
    return pl.pallas_call(
        _out_kernel,
        grid=grid,
        in_specs=[tok(DIFF_WIDTH), tok(SB_WIDTH), tok(d),
                  pl.BlockSpec((bb, 6, d), lambda i, t: (i + mod_off // bb, 0, 0)),
                  const((1, d)), const((1, d)), const((1, d)),
                  const((MIX_WIDTH, d)), const((d, D_FF)), const((D_FF, d))],
        out_specs=tok(d),
        out_shape=jax.ShapeDtypeStruct((b, s, d), F32),
        compiler_params=pltpu.CompilerParams(dimension_semantics=("parallel", "parallel"),
                                             vmem_limit_bytes=VMEM_LIMIT),
        name="out_ffn",
    )(od, osb, x, mods, g_post_mix, g_pre_ffn, g_post_ffn, wo_bf, wu_bf, wd_bf)


def kernel(x_prompt, x_sample, c_prompt, c_sample, cache_diff_k, cache_diff_v, cache_sb_k, cache_sb_v,
           w_ada, b_ada, g_pre_mix, g_post_mix, w_in, lambda_q1, lambda_k1, lambda_q2, lambda_k2,
           diff_subln_g, w_out, g_pre_ffn, g_post_ffn, w_up, w_down):
    bp, sp, d = x_prompt.shape
    bs, ss, _ = x_sample.shape
    past = cache_diff_k.shape[2]
    l = 0

    c_all = jnp.concatenate([c_prompt, c_sample], axis=0)
    mods, lam_tile = _modulation(c_all, w_ada[l], b_ada[l][None, :],
                                 lambda_q1[l][None, :], lambda_k1[l][None, :],
                                 lambda_q2[l][None, :], lambda_k2[l][None, :])
    mods = mods.reshape(bp + bs, 6, d)
    lam = lam_tile[0, 0:1]
    slopes = jnp.asarray(ALIBI_SLOPES, F32)
    subln_g = diff_subln_g[l].reshape(SLAB, 1)

    w_in_bf = w_in[l].astype(BF16)
    w_out_bf = w_out[l].astype(BF16)
    w_up_bf = w_up[l].astype(BF16)
    w_down_bf = w_down[l].astype(BF16)
    g1 = g_pre_mix[l][None, :]
    g2 = g_post_mix[l][None, :]
    g3 = g_pre_ffn[l][None, :]
    g4 = g_post_ffn[l][None, :]

    q, kb, vb, kd, vd, ks, vs = _in_proj(x_prompt, mods, 0, g1, w_in_bf, bb=1, tm=512)
    od, osb = _attention(q, (kb, kb), (vb, vb), kb, vb, slopes, lam, subln_g,
                         tq=256, tk=256, td=256, causal_blocks=True, past=0)
    y_prompt = _out_ffn(od, osb, x_prompt, mods, 0, g2, g3, g4,
                        w_out_bf, w_up_bf, w_down_bf, bb=1, tm=512)
    prompt_kv = [a.reshape(1, bp, sp, *tail) for a, tail in
                 ((kd, (DIFF_HEADS, 2 * HEAD_DIM)), (vd, (DIFF_HEADS, 2 * HEAD_DIM)),
                  (ks, (SB_HEADS, HEAD_DIM)), (vs, (SB_HEADS, HEAD_DIM)))]

    q2, kb2, vb2, kd2, vd2, ks2, vs2 = _in_proj(x_sample, mods, bp, g1, w_in_bf, bb=4, tm=ss)
    cdk = cache_diff_k[l].reshape(bs, past, DIFF_WIDTH)
    cdv = cache_diff_v[l].reshape(bs, past, DIFF_WIDTH)
    csk = cache_sb_k[l].reshape(bs, past, SB_WIDTH)
    csv = cache_sb_v[l].reshape(bs, past, SB_WIDTH)
    od2, osb2 = _attention(q2, (cdk, csk), (cdv, csv), kb2, vb2, slopes, lam, subln_g,
                           tq=ss, tk=256, td=ss, causal_blocks=False, past=past)
    y_sample = _out_ffn(od2, osb2, x_sample, mods, bp, g2, g3, g4,
                        w_out_bf, w_up_bf, w_down_bf, bb=4, tm=ss)
    sample_kv = [a.reshape(1, bs, ss, *tail) for a, tail in
                 ((kd2, (DIFF_HEADS, 2 * HEAD_DIM)), (vd2, (DIFF_HEADS, 2 * HEAD_DIM)),
                  (ks2, (SB_HEADS, HEAD_DIM)), (vs2, (SB_HEADS, HEAD_DIM)))]

    return (y_prompt, y_sample, *prompt_kv, *sample_kv)
```

```python
import functools
import math

import jax
import jax.numpy as jnp
import numpy as np
from jax import lax
from jax.experimental import pallas as pl
from jax.experimental.pallas import tpu as pltpu

D_MODEL = 1024
CHUNK = 64
HEAD_DIM = 64
DIFF_HEADS = 4
SB_HEADS = 8
SLAB = 128
DIFF_WIDTH = DIFF_HEADS * 2 * HEAD_DIM
SB_WIDTH = SB_HEADS * HEAD_DIM
MIX_WIDTH = DIFF_WIDTH + SB_WIDTH
IN_WIDTH = 3 * MIX_WIDTH
D_FF = 4 * D_MODEL
EPS = 1e-6
NEG = -1e30
LAMBDA_INIT = 0.8 - 0.6 * math.exp(-0.3 * 0)
ATTN_SCALE = HEAD_DIM ** -0.5
LOG2E = math.log2(math.e)
Q_SCALE = ATTN_SCALE * LOG2E
ALIBI_SLOPES = tuple(float(2.0 ** (-8.0 * (i + 1) / DIFF_HEADS)) for i in range(DIFF_HEADS))

VMEM_LIMIT = 56 * 1024 * 1024
BF16 = jnp.bfloat16
F32 = jnp.float32

NT_DIMS = (((1,), (1,)), ((), ()))
TN_DIMS = (((0,), (0,)), ((), ()))


def _rms(x, g):
    ms = jnp.mean(x * x, axis=-1, keepdims=True)
    return x * lax.rsqrt(ms + EPS) * g


def _mod_kernel(c_ref, w_ref, b_ref, lq1_ref, lk1_ref, lq2_ref, lk2_ref, m_ref, lam_ref):
    c = c_ref[...]
    s = c * jax.nn.sigmoid(c)
    m_ref[...] = jnp.dot(s, w_ref[...], preferred_element_type=F32,
                         precision=lax.Precision.HIGHEST) + b_ref[...]
    d1 = jnp.sum(lq1_ref[...] * lk1_ref[...], axis=-1, keepdims=True)
    d2 = jnp.sum(lq2_ref[...] * lk2_ref[...], axis=-1, keepdims=True)
    lam = jnp.exp(d1) - jnp.exp(d2) + LAMBDA_INIT
    lam_ref[...] = jnp.broadcast_to(lam, lam_ref.shape)


def _modulation(c_all, w_ada, b_ada, lq1, lk1, lq2, lk2):
    nb = c_all.shape[0]
    tn = 1024
    vec = pl.BlockSpec((1, HEAD_DIM), lambda j: (0, 0))
    return pl.pallas_call(
        _mod_kernel,
        grid=(6 * D_MODEL // tn,),
        in_specs=[pl.BlockSpec((nb, D_MODEL), lambda j: (0, 0)),
                  pl.BlockSpec((D_MODEL, tn), lambda j: (0, j)),
                  pl.BlockSpec((1, tn), lambda j: (0, j)),
                  vec, vec, vec, vec],
        out_specs=[pl.BlockSpec((nb, tn), lambda j: (0, j)),
                   pl.BlockSpec((8, 128), lambda j: (0, 0))],
        out_shape=[jax.ShapeDtypeStruct((nb, 6 * D_MODEL), F32),
                   jax.ShapeDtypeStruct((8, 128), F32)],
        compiler_params=pltpu.CompilerParams(dimension_semantics=("arbitrary",),
                                             vmem_limit_bytes=VMEM_LIMIT),
        name="modulation",
    )(c_all, w_ada, b_ada, lq1, lk1, lq2, lk2)


def _in_kernel(x_ref, mod_ref, g_ref, w_ref,
               q_ref, kb_ref, vb_ref, kd_ref, vd_ref, ks_ref, vs_ref):
    bb, tm, d = x_ref.shape
    x = x_ref[...]
    shift = mod_ref[:, 0:1, :]
    scale = mod_ref[:, 1:2, :]
    h = _rms(x, g_ref[...]) * (1.0 + scale) + shift
    hb = h.reshape(bb * tm, d).astype(BF16)

    def proj(c):
        u = jnp.dot(hb, w_ref[:, c * 512:(c + 1) * 512], preferred_element_type=F32)
        return u.reshape(bb, tm, 512)

    q_ref[:, :, 0:512] = (proj(0) * Q_SCALE).astype(BF16)
    q_ref[:, :, 512:1024] = (proj(3) * Q_SCALE).astype(BF16)
    for c, f_ref, b_ref, lo in ((1, kd_ref, kb_ref, 0), (2, vd_ref, vb_ref, 0),
                                (4, ks_ref, kb_ref, 512), (5, vs_ref, vb_ref, 512)):
        u = proj(c)
        f_ref[...] = u
        b_ref[:, :, lo:lo + 512] = u.astype(BF16)


def _in_proj(x, mods, mod_off, g, w_bf, bb, tm):
    b, s, d = x.shape
    grid = (b // bb, s // tm)
    tok = lambda w: pl.BlockSpec((bb, tm, w), lambda i, t: (i, t, 0))
    const = lambda shape: pl.BlockSpec(shape, lambda i, t: (0,) * len(shape),
                                       pipeline_mode=pl.Buffered(1))
    outs = [jax.ShapeDtypeStruct((b, s, MIX_WIDTH), BF16)] * 3 + \
           [jax.ShapeDtypeStruct((b, s, 512), F32)] * 4
    return pl.pallas_call(
        _in_kernel,
        grid=grid,
        in_specs=[tok(d),
                  pl.BlockSpec((bb, 6, d), lambda i, t: (i + mod_off // bb, 0, 0)),
                  const((1, d)),
                  const((d, IN_WIDTH))],
        out_specs=[tok(MIX_WIDTH)] * 3 + [tok(512)] * 4,
        out_shape=outs,
        compiler_params=pltpu.CompilerParams(dimension_semantics=("parallel", "parallel"),
                                             vmem_limit_bytes=VMEM_LIMIT),
        name="in_proj",
    )(x, mods, g, w_bf)


def _half_masks(q):
    lane = lax.broadcasted_iota(jnp.int32, q.shape, 1)
    zero = jnp.zeros_like(q)
    return jnp.where(lane < HEAD_DIM, q, zero), jnp.where(lane >= HEAD_DIM, q, zero)


def _positions(n, tq, k0, q0):
    kpos = k0 + lax.broadcasted_iota(jnp.int32, (n, tq), 0)
    qpos = q0 + lax.broadcasted_iota(jnp.int32, (n, tq), 1)
    return kpos, qpos


def _old_run_pipeline(front_main, front_diag, back, state, *, n_main, diag_in_main):
    if diag_in_main:
        staged = front_diag()

        def body(i, carry):
            staged, state = carry
            nxt = front_main(n_main - 1 - i)
            return nxt, back(staged, state, False)

        staged, state = lax.fori_loop(0, n_main, body, (staged, state))
        return back(staged, state, False)

    state = back(front_diag(), state, True)
    staged = front_main(n_main - 1)

    def body(i, carry):
        staged, state = carry
        nxt = front_main(n_main - 1 - i)
        return nxt, back(staged, state, False)

    staged, state = lax.fori_loop(1, n_main, body, (staged, state))
    return back(staged, state, False)


def _old_diff_kernel(*refs, tk, td, n_main, q_base, diag_base, diag_in_main):
    if diag_in_main:
        lam_ref, q_ref, km_ref, vm_ref, g_ref, o_ref = refs
        kd_ref = vd_ref = None
    else:
        lam_ref, q_ref, km_ref, vm_ref, kd_ref, vd_ref, g_ref, o_ref = refs
    tq = q_ref.shape[1]
    qi = pl.program_id(1)
    lam = lam_ref[0]
    q0 = q_base(qi)
    nm = n_main(qi)
    heads = range(DIFF_HEADS)
    slab = lambda h: slice(h * SLAB, (h + 1) * SLAB)
    slopes = [s * LOG2E for s in ALIBI_SLOPES]
    qs = [_half_masks(q_ref[0, :, slab(h)]) for h in heads]

    rel = (lax.broadcasted_iota(jnp.int32, (tk, tq), 1)
           - lax.broadcasted_iota(jnp.int32, (tk, tq), 0)).astype(F32)

    def front_main(j):
        k0 = pl.multiple_of(j * tk, tk)
        dist0 = (q0 - k0).astype(F32)
        out = []
        for h in heads:
            kblk = km_ref[0, pl.ds(k0, tk), slab(h)].astype(BF16)
            bias = -slopes[h] * rel
            out.append(lax.dot_general(kblk, qs[h][0], NT_DIMS, preferred_element_type=F32) + bias)
            out.append(lax.dot_general(kblk, qs[h][1], NT_DIMS, preferred_element_type=F32) + bias)
        return tuple(out) + (dist0, jnp.asarray(j, jnp.int32))

    def front_diag():
        kpos, qpos = _positions(td, tq, diag_base(qi), q0)
        visible = (kpos // CHUNK) <= (qpos // CHUNK)
        dist = jnp.abs(qpos - kpos).astype(F32)
        out = []
        for h in heads:
            if diag_in_main:
                kblk = km_ref[0, pl.ds(pl.multiple_of(qi * tk, tk), tk), slab(h)].astype(BF16)
            else:
                kblk = kd_ref[0, :, slab(h)].astype(BF16)
            bias = -slopes[h] * dist
            for qm in qs[h]:
                s = lax.dot_general(kblk, qm, NT_DIMS, preferred_element_type=F32) + bias
                out.append(jnp.where(visible, s, NEG))
        return tuple(out) + (jnp.zeros((), F32), jnp.asarray(qi, jnp.int32))

    def back(staged, state, from_diag_ref):
        dist0, j = staged[-2:]
        new_state = []
        for h in heads:
            if from_diag_ref:
                vblk = vd_ref[0, :, slab(h)].astype(BF16)
            else:
                vblk = vm_ref[0, pl.ds(pl.multiple_of(j * tk, tk), tk), slab(h)].astype(BF16)
            off = slopes[h] * dist0
            for mi in range(2):
                s = staged[2 * h + mi]
                m, l, acc = state[2 * h + mi]
                m_new = jnp.maximum(m, jnp.max(s, axis=0, keepdims=True) - off)
                p = jnp.exp2(s - (m_new + off))
                alpha = jnp.exp2(m - m_new)
                l_new = alpha * l + jnp.sum(p, axis=0, keepdims=True)
                pv = lax.dot_general(vblk, p.astype(BF16), TN_DIMS, preferred_element_type=F32)
                new_state.append((m_new, l_new, alpha * acc + pv))
        return tuple(new_state)

    init = tuple((jnp.full((1, tq), NEG, F32), jnp.zeros((1, tq), F32), jnp.zeros((SLAB, tq), F32))
                 for _ in range(2 * DIFF_HEADS))
    state = _run_pipeline(front_main, front_diag, back, init, n_main=nm, diag_in_main=diag_in_main)

    for h in heads:
        st1, st2 = state[2 * h], state[2 * h + 1]
        o = st1[2] / st1[1] - lam * (st2[2] / st2[1])
        ms = jnp.mean(o * o, axis=0, keepdims=True)
        o = o * lax.rsqrt(ms + EPS) * g_ref[...] * (1.0 - LAMBDA_INIT)
        o_ref[0, :, slab(h)] = o.T.astype(o_ref.dtype)


def _old_sb_kernel(*refs, tk, td, n_main, q_base, diag_base, diag_in_main):
    if diag_in_main:
        q_ref, km_ref, vm_ref, um_ref, o_ref = refs
        kd_ref = vd_ref = ud_ref = None
    else:
        q_ref, km_ref, vm_ref, kd_ref, vd_ref, um_ref, ud_ref, o_ref = refs
    tq = q_ref.shape[1]
    qi = pl.program_id(1)
    q0 = q_base(qi)
    nm = n_main(qi)
    slabs = range(SB_HEADS // 2)
    slab = lambda s: slice(s * SLAB, (s + 1) * SLAB)
    qs = [_half_masks(q_ref[0, :, slab(s)]) for s in slabs]

    def stage(z, earlier):
        sp = jnp.log(1.0 + jnp.exp2(-jnp.abs(z))) * LOG2E
        t = jnp.maximum(z, 0.0) + sp
        zt = z - t
        if earlier is not None:
            t = jnp.where(earlier, t, 0.0)
            zt = jnp.where(earlier, zt, NEG)
        t_hi = t.astype(BF16)
        t_lo = (t - t_hi.astype(F32)).astype(BF16)
        return zt, jnp.concatenate([t_hi, t_lo], axis=0), t[0:1, :]

    def front(kblks, earlier, j):
        out = []
        for s in slabs:
            for qm in qs[s]:
                z = lax.dot_general(kblks[s], qm, NT_DIMS, preferred_element_type=F32)
                out.extend(stage(z, earlier))
        return tuple(out) + (jnp.asarray(j, jnp.int32),)

    def front_main(j):
        k0 = pl.multiple_of(j * tk, tk)
        return front([km_ref[0, pl.ds(k0, tk), slab(s)].astype(BF16) for s in slabs], None, j)

    def front_diag():
        if diag_in_main:
            k0 = pl.multiple_of(qi * tk, tk)
            kblks = [km_ref[0, pl.ds(k0, tk), slab(s)].astype(BF16) for s in slabs]
        else:
            kblks = [kd_ref[0, :, slab(s)].astype(BF16) for s in slabs]
        kpos, qpos = _positions(td, tq, diag_base(qi), q0)
        return front(kblks, kpos < qpos, qi)

    def back(staged, state, from_diag_ref):
        j = staged[-1]
        u2 = ud_ref[...] if from_diag_ref else um_ref[...]
        new_state = []
        for s in slabs:
            if from_diag_ref:
                vblk = vd_ref[0, :, slab(s)].astype(BF16)
            else:
                vblk = vm_ref[0, pl.ds(pl.multiple_of(j * tk, tk), tk), slab(s)].astype(BF16)
            for hi in range(2):
                zt, tt, t0 = staged[3 * (2 * s + hi):3 * (2 * s + hi) + 3]
                carry, acc = state[2 * s + hi]
                later = jnp.dot(u2, tt, preferred_element_type=F32)
                w = jnp.exp2(zt - later - carry)
                pv = lax.dot_general(vblk, w.astype(BF16), TN_DIMS, preferred_element_type=F32)
                new_state.append((carry + later[0:1, :] + t0, acc + pv))
        return tuple(new_state)

    init = tuple((jnp.zeros((1, tq), F32), jnp.zeros((SLAB, tq), F32)) for _ in range(SB_HEADS))
    state = _run_pipeline(front_main, front_diag, back, init, n_main=nm, diag_in_main=diag_in_main)
    row = lax.broadcasted_iota(jnp.int32, (SLAB, tq), 0)
    for s in slabs:
        o = jnp.where(row < HEAD_DIM, state[2 * s][1], state[2 * s + 1][1])
        o_ref[0, :, slab(s)] = o.T.astype(o_ref.dtype)


def _staging(refs, slot):
    def sink(idx, *vals):
        for r, v in zip(refs, vals):
            r[slot, idx] = v

    def source(idx, k):
        return refs[k][slot, idx]
    return sink, source


def _list_staging():
    store = {}

    def sink(idx, *vals):
        store[idx] = vals

    def source(idx, k):
        return store[idx][k]
    return sink, source


def _run_pipeline(stages, stage_refs, state, *, n_main, diag_in_main):
    front_mm, front_vpu, back_mm, back_vpu = stages

    def step(i, j_next, src_prev, sc_prev, state):
        sink, _ = _staging(stage_refs, (i + 1) % 2)
        zs = front_mm(j_next)
        pre = back_mm(src_prev, False)
        sc_next = front_vpu(zs, j_next, sink)
        return sc_next, back_vpu(src_prev, pre, sc_prev, state, False)

    if diag_in_main:
        sink, _ = _staging(stage_refs, 0)
        scalars = front_vpu(front_mm(None), None, sink)

        def body(i, carry):
            scalars, state = carry
            _, src = _staging(stage_refs, i % 2)
            return step(i, n_main - 1 - i, src, scalars, state)

        scalars, state = lax.fori_loop(0, n_main, body, (scalars, state))
        _, src = _staging(stage_refs, n_main % 2)
        return back_vpu(src, back_mm(src, False), scalars, state, False)

    sink, src = _list_staging()
    scalars = front_vpu(front_mm(None), None, sink)
    state = back_vpu(src, back_mm(src, True), scalars, state, True)
    sink, _ = _staging(stage_refs, 1)
    scalars = front_vpu(front_mm(n_main - 1), n_main - 1, sink)

    def body(i, carry):
        scalars, state = carry
        _, src = _staging(stage_refs, i % 2)
        return step(i, n_main - 1 - i, src, scalars, state)

    scalars, state = lax.fori_loop(1, n_main, body, (scalars, state))
    _, src = _staging(stage_refs, n_main % 2)
    return back_vpu(src, back_mm(src, False), scalars, state, False)


def _diff_kernel(*refs, tk, td, n_main, q_base, diag_base, diag_in_main):
    if diag_in_main:
        lam_ref, q_ref, km_ref, vm_ref, g_ref, o_ref, s_ref = refs
        kd_ref = vd_ref = None
    else:
        lam_ref, q_ref, km_ref, vm_ref, kd_ref, vd_ref, g_ref, o_ref, s_ref = refs
    tq = q_ref.shape[1]
    qi = pl.program_id(1)
    lam = lam_ref[0]
    q0 = q_base(qi)
    nm = n_main(qi)
    heads = range(DIFF_HEADS)
    slab = lambda h: slice(h * SLAB, (h + 1) * SLAB)
    slopes = [s * LOG2E for s in ALIBI_SLOPES]
    qs = [_half_masks(q_ref[0, :, slab(h)]) for h in heads]

    rel = (lax.broadcasted_iota(jnp.int32, (tk, tq), 1)
           - lax.broadcasted_iota(jnp.int32, (tk, tq), 0)).astype(F32)

    def front_mm(j):
        zs = []
        for h in heads:
            if j is not None:
                kblk = km_ref[0, pl.ds(pl.multiple_of(j * tk, tk), tk), slab(h)].astype(BF16)
            elif diag_in_main:
                kblk = km_ref[0, pl.ds(pl.multiple_of(qi * tk, tk), tk), slab(h)].astype(BF16)
            else:
                kblk = kd_ref[0, :, slab(h)].astype(BF16)
            for qm in qs[h]:
                zs.append(lax.dot_general(kblk, qm, NT_DIMS, preferred_element_type=F32))
        return zs

    def front_vpu(zs, j, sink):
        if j is not None:
            for h in heads:
                bias = -slopes[h] * rel
                for mi in range(2):
                    sink(2 * h + mi, zs[2 * h + mi] + bias)
            return (q0 - j * tk).astype(F32), jnp.asarray(j, jnp.int32)
        kpos, qpos = _positions(td, tq, diag_base(qi), q0)
        visible = (kpos // CHUNK) <= (qpos // CHUNK)
        dist = jnp.abs(qpos - kpos).astype(F32)
        for h in heads:
            bias = -slopes[h] * dist
            for mi in range(2):
                sink(2 * h + mi, jnp.where(visible, zs[2 * h + mi] + bias, NEG))
        return jnp.zeros((), F32), jnp.asarray(qi, jnp.int32)

    def back_mm(source, diag_ref):
        return None

    def back_vpu(source, pre, scalars, state, diag_ref):
        dist0, j = scalars
        new_state = []
        for h in heads:
            if diag_ref:
                vblk = vd_ref[0, :, slab(h)].astype(BF16)
            else:
                vblk = vm_ref[0, pl.ds(pl.multiple_of(j * tk, tk), tk), slab(h)].astype(BF16)
            off = slopes[h] * dist0
            for mi in range(2):
                s = source(2 * h + mi, 0)
                m, l, acc = state[2 * h + mi]
                m_new = jnp.maximum(m, jnp.max(s, axis=0, keepdims=True) - off)
                p = jnp.exp2(s - (m_new + off))
                alpha = jnp.exp2(m - m_new)
                l_new = alpha * l + jnp.sum(p, axis=0, keepdims=True)
                pv = lax.dot_general(vblk, p.astype(BF16), TN_DIMS, preferred_element_type=F32)
                new_state.append((m_new, l_new, alpha * acc + pv))
        return tuple(new_state)

    init = tuple((jnp.full((1, tq), NEG, F32), jnp.zeros((1, tq), F32), jnp.zeros((SLAB, tq), F32))
                 for _ in range(2 * DIFF_HEADS))
    state = _run_pipeline((front_mm, front_vpu, back_mm, back_vpu), (s_ref,), init,
                          n_main=nm, diag_in_main=diag_in_main)

    for h in heads:
        st1, st2 = state[2 * h], state[2 * h + 1]
        o = st1[2] / st1[1] - lam * (st2[2] / st2[1])
        ms = jnp.mean(o * o, axis=0, keepdims=True)
        o = o * lax.rsqrt(ms + EPS) * g_ref[...] * (1.0 - LAMBDA_INIT)
        o_ref[0, :, slab(h)] = o.T.astype(o_ref.dtype)


def _sb_kernel(*refs, tk, td, n_main, q_base, diag_base, diag_in_main):
    if diag_in_main:
        q_ref, km_ref, vm_ref, um_ref, o_ref, zt_ref, tt_ref, t0_ref = refs
        kd_ref = vd_ref = ud_ref = None
    else:
        q_ref, km_ref, vm_ref, kd_ref, vd_ref, um_ref, ud_ref, o_ref, zt_ref, tt_ref, t0_ref = refs
    tq = q_ref.shape[1]
    qi = pl.program_id(1)
    q0 = q_base(qi)
    nm = n_main(qi)
    slabs = range(SB_HEADS // 2)
    slab = lambda s: slice(s * SLAB, (s + 1) * SLAB)
    qs = [_half_masks(q_ref[0, :, slab(s)]) for s in slabs]

    def front_mm(j):
        zs = []
        for s in slabs:
            if j is not None:
                kblk = km_ref[0, pl.ds(pl.multiple_of(j * tk, tk), tk), slab(s)].astype(BF16)
            elif diag_in_main:
                kblk = km_ref[0, pl.ds(pl.multiple_of(qi * tk, tk), tk), slab(s)].astype(BF16)
            else:
                kblk = kd_ref[0, :, slab(s)].astype(BF16)
            for qm in qs[s]:
                zs.append(lax.dot_general(kblk, qm, NT_DIMS, preferred_element_type=F32))
        return zs

    def front_vpu(zs, j, sink):
        earlier = None
        if j is None:
            kpos, qpos = _positions(td, tq, diag_base(qi), q0)
            earlier = kpos < qpos
        for idx, z in enumerate(zs):
            sp = jnp.log(1.0 + jnp.exp2(-jnp.abs(z))) * LOG2E
            t = jnp.maximum(z, 0.0) + sp
            zt = z - t
            if earlier is not None:
                t = jnp.where(earlier, t, 0.0)
                zt = jnp.where(earlier, zt, NEG)
            t_hi = t.astype(BF16)
            t_lo = (t - t_hi.astype(F32)).astype(BF16)
            sink(idx, zt, jnp.concatenate([t_hi, t_lo], axis=0), t[0:1, :])
        return (jnp.asarray(qi if j is None else j, jnp.int32),)

    def back_mm(source, diag_ref):
        u2 = ud_ref[...] if diag_ref else um_ref[...]
        return [jnp.dot(u2, source(idx, 1), preferred_element_type=F32) for idx in range(SB_HEADS)]

    def back_vpu(source, laters, scalars, state, diag_ref):
        (j,) = scalars
        new_state = []
        for s in slabs:
            if diag_ref:
                vblk = vd_ref[0, :, slab(s)].astype(BF16)
            else:
                vblk = vm_ref[0, pl.ds(pl.multiple_of(j * tk, tk), tk), slab(s)].astype(BF16)
            for hi in range(2):
                idx = 2 * s + hi
                zt, t0 = source(idx, 0), source(idx, 2)
                carry, acc = state[idx]
                later = laters[idx]
                w = jnp.exp2(zt - later - carry)
                pv = lax.dot_general(vblk, w.astype(BF16), TN_DIMS, preferred_element_type=F32)
                new_state.append((carry + later[0:1, :] + t0, acc + pv))
        return tuple(new_state)

    init = tuple((jnp.zeros((1, tq), F32), jnp.zeros((SLAB, tq), F32)) for _ in range(SB_HEADS))
    state = _run_pipeline((front_mm, front_vpu, back_mm, back_vpu), (zt_ref, tt_ref, t0_ref), init,
                          n_main=nm, diag_in_main=diag_in_main)
    row = lax.broadcasted_iota(jnp.int32, (SLAB, tq), 0)
    for s in slabs:
        o = jnp.where(row < HEAD_DIM, state[2 * s][1], state[2 * s + 1][1])
        o_ref[0, :, slab(s)] = o.T.astype(o_ref.dtype)


def _strict_upper_pair(n):
    r = np.arange(n)
    u = (r[None, :] > r[:, None]).astype(np.float32)
    return jnp.asarray(np.concatenate([u, u], axis=1), dtype=BF16)


def _attention(q, k_main, v_main, k_diag, v_diag, lam, subln_g, *,
               tq, tk, td, causal_blocks, past):
    b, t, _ = q.shape
    nq = t // tq
    grid = (b, nq)
    if causal_blocks:
        assert tq == tk == td
        n_main = lambda qi: qi
        q_base = lambda qi: qi * tq
        diag_base = lambda qi: qi * tq
    else:
        n_main = lambda qi: past // tk
        q_base = lambda qi: past + qi * tq
        diag_base = lambda qi: past + qi * tq
    kw = dict(tk=tk, td=td, n_main=n_main, q_base=q_base, diag_base=diag_base,
              diag_in_main=causal_blocks)
    cparams = pltpu.CompilerParams(dimension_semantics=("parallel", "arbitrary"),
                                   vmem_limit_bytes=VMEM_LIMIT)

    (kmd, kms), (vmd, vms) = k_main, v_main
    tm_len = kmd.shape[1]
    width = DIFF_WIDTH

    def specs(half_main, half_new):
        q_spec = pl.BlockSpec((1, tq, width), lambda i, j: (i, j, half_new))
        main = pl.BlockSpec((1, tm_len, width), lambda i, j: (i, 0, half_main))
        diag = pl.BlockSpec((1, td, width), lambda i, j: (i, j, half_new))
        return q_spec, main, diag

    smem = pl.BlockSpec(memory_space=pltpu.SMEM)
    out_spec = pl.BlockSpec((1, tq, width), lambda i, j: (i, j, 0))
    out_shape = jax.ShapeDtypeStruct((b, t, width), BF16)
    const2 = lambda n: pl.BlockSpec((n, 2 * n), lambda i, j: (0, 0))
    g_spec = pl.BlockSpec((SLAB, 1), lambda i, j: (0, 0))

    q_spec, main_spec, diag_spec = specs(0, 0)
    if causal_blocks:
        in_specs, args = [smem, q_spec, main_spec, main_spec, g_spec], \
                         (lam, q, kmd, vmd, subln_g)
    else:
        in_specs, args = [smem, q_spec, main_spec, main_spec, diag_spec, diag_spec, g_spec], \
                         (lam, q, kmd, vmd, k_diag, v_diag, subln_g)
    od = pl.pallas_call(
        functools.partial(_diff_kernel, **kw),
        grid=grid, in_specs=in_specs, out_specs=out_spec, out_shape=out_shape,
        scratch_shapes=[pltpu.VMEM((2, 2 * DIFF_HEADS, tk, tq), F32)],
        compiler_params=cparams, name="diff_attention",
    )(*args)

    q_spec, main_spec, diag_spec = specs(1 if causal_blocks else 0, 1)
    if causal_blocks:
        in_specs, args = [q_spec, main_spec, main_spec, const2(tk)], \
                         (q, kms, vms, _strict_upper_pair(tk))
    else:
        in_specs, args = [q_spec, main_spec, main_spec, diag_spec, diag_spec, const2(tk), const2(td)], \
                         (q, kms, vms, k_diag, v_diag, _strict_upper_pair(tk), _strict_upper_pair(td))
    osb = pl.pallas_call(
        functools.partial(_sb_kernel, **kw),
        grid=grid, in_specs=in_specs, out_specs=out_spec, out_shape=out_shape,
        scratch_shapes=[pltpu.VMEM((2, SB_HEADS, tk, tq), F32),
                        pltpu.VMEM((2, SB_HEADS, 2 * tk, tq), BF16),
                        pltpu.VMEM((2, SB_HEADS, 1, tq), F32)],
        compiler_params=cparams, name="sb_attention",
    )(*args)
    return od, osb


def _out_kernel(od_ref, osb_ref, x_ref, mod_ref, gpm_ref, gpf_ref, gqf_ref,
                wo_ref, wu_ref, wd_ref, y_ref):
    bb, tm, d = x_ref.shape
    n = bb * tm
    od = od_ref[...].reshape(n, DIFF_WIDTH)
    osb = osb_ref[...].reshape(n, SB_WIDTH)
    y = (jnp.dot(od, wo_ref[0:DIFF_WIDTH, :], preferred_element_type=F32)
         + jnp.dot(osb, wo_ref[DIFF_WIDTH:MIX_WIDTH, :], preferred_element_type=F32))
    y = y.reshape(bb, tm, d)
    gate1 = mod_ref[:, 2:3, :]
    shift2 = mod_ref[:, 3:4, :]
    scale2 = mod_ref[:, 4:5, :]
    gate2 = mod_ref[:, 5:6, :]
    x1 = x_ref[...] + gate1 * _rms(y, gpm_ref[...])
    h2 = _rms(x1, gpf_ref[...]) * (1.0 + scale2) + shift2
    h2b = h2.reshape(n, d).astype(BF16)
    acc = jnp.zeros((n, d), F32)
    fc = 1024
    for c in range(D_FF // fc):
        f = jnp.dot(h2b, wu_ref[:, c * fc:(c + 1) * fc], preferred_element_type=F32)
        r = jnp.square(jnp.maximum(f, 0.0)).astype(BF16)
        acc = acc + jnp.dot(r, wd_ref[c * fc:(c + 1) * fc, :], preferred_element_type=F32)
    y2 = acc.reshape(bb, tm, d)
    y_ref[...] = x1 + gate2 * _rms(y2, gqf_ref[...])


def _out_ffn(od, osb, x, mods, mod_off, g_post_mix, g_pre_ffn, g_post_ffn,
             wo_bf, wu_bf, wd_bf, bb, tm):
    b, s, d = x.shape
    grid = (b // bb, s // tm)
    tok = lambda w: pl.BlockSpec((bb, tm, w), lambda i, t: (i, t, 0))
    const = lambda shape: pl.BlockSpec(shape, lambda i, t: (0,) * len(shape),
                                       pipeline_mode=pl.Buffered(1))
    return pl.pallas_call(
        _out_kernel,
        grid=grid,
        in_specs=[tok(DIFF_WIDTH), tok(SB_WIDTH), tok(d),
                  pl.BlockSpec((bb, 6, d), lambda i, t: (i + mod_off // bb, 0, 0)),
                  const((1, d)), const((1, d)), const((1, d)),
                  const((MIX_WIDTH, d)), const((d, D_FF)), const((D_FF, d))],
        out_specs=tok(d),
        out_shape=jax.ShapeDtypeStruct((b, s, d), F32),
        compiler_params=pltpu.CompilerParams(dimension_semantics=("parallel", "parallel"),
                                             vmem_limit_bytes=VMEM_LIMIT),
        name="out_ffn",
    )(od, osb, x, mods, g_post_mix, g_pre_ffn, g_post_ffn, wo_bf, wu_bf, wd_bf)


def kernel(x_prompt, x_sample, c_prompt, c_sample, cache_diff_k, cache_diff_v, cache_sb_k, cache_sb_v,
           w_ada, b_ada, g_pre_mix, g_post_mix, w_in, lambda_q1, lambda_k1, lambda_q2, lambda_k2,
           diff_subln_g, w_out, g_pre_ffn, g_post_ffn, w_up, w_down):
    bp, sp, d = x_prompt.shape
    bs, ss, _ = x_sample.shape
    past = cache_diff_k.shape[2]
    l = 0

    c_all = jnp.concatenate([c_prompt, c_sample], axis=0)
    mods, lam_tile = _modulation(c_all, w_ada[l], b_ada[l][None, :],
                                 lambda_q1[l][None, :], lambda_k1[l][None, :],
                                 lambda_q2[l][None, :], lambda_k2[l][None, :])
    mods = mods.reshape(bp + bs, 6, d)
    lam = lam_tile[0, 0:1]
    subln_g = diff_subln_g[l].reshape(SLAB, 1)

    w_in_bf = w_in[l].astype(BF16)
    w_out_bf = w_out[l].astype(BF16)
    w_up_bf = w_up[l].astype(BF16)
    w_down_bf = w_down[l].astype(BF16)
    g1 = g_pre_mix[l][None, :]
    g2 = g_post_mix[l][None, :]
    g3 = g_pre_ffn[l][None, :]
    g4 = g_post_ffn[l][None, :]

    q, kb, vb, kd, vd, ks, vs = _in_proj(x_prompt, mods, 0, g1, w_in_bf, bb=1, tm=512)
    od, osb = _attention(q, (kb, kb), (vb, vb), kb, vb, lam, subln_g,
                         tq=256, tk=256, td=256, causal_blocks=True, past=0)
    y_prompt = _out_ffn(od, osb, x_prompt, mods, 0, g2, g3, g4,
                        w_out_bf, w_up_bf, w_down_bf, bb=1, tm=512)
    prompt_kv = [a.reshape(1, bp, sp, *tail) for a, tail in
                 ((kd, (DIFF_HEADS, 2 * HEAD_DIM)), (vd, (DIFF_HEADS, 2 * HEAD_DIM)),
                  (ks, (SB_HEADS, HEAD_DIM)), (vs, (SB_HEADS, HEAD_DIM)))]

    q2, kb2, vb2, kd2, vd2, ks2, vs2 = _in_proj(x_sample, mods, bp, g1, w_in_bf, bb=4, tm=ss)
    cdk = cache_diff_k[l].reshape(bs, past, DIFF_WIDTH)
    cdv = cache_diff_v[l].reshape(bs, past, DIFF_WIDTH)
    csk = cache_sb_k[l].reshape(bs, past, SB_WIDTH)
    csv = cache_sb_v[l].reshape(bs, past, SB_WIDTH)
    od2, osb2 = _attention(q2, (cdk, csk), (cdv, csv), kb2, vb2, lam, subln_g,
                           tq=ss, tk=256, td=ss, causal_blocks=False, past=past)
    y_sample = _out_ffn(od2, osb2, x_sample, mods, bp, g2, g3, g4,
                        w_out_bf, w_up_bf, w_down_bf, bb=4, tm=ss)
    sample_kv = [a.reshape(1, bs, ss, *tail) for a, tail in
                 ((kd2, (DIFF_HEADS, 2 * HEAD_DIM)), (vd2, (DIFF_HEADS, 2 * HEAD_DIM)),
                  (ks2, (SB_HEADS, HEAD_DIM)), (vs2, (SB_HEADS, HEAD_DIM)))]

    return (y_prompt, y_sample, *prompt_kv, *sample_kv)
```

```python
import functools
import math

import jax
import jax.numpy as jnp
import numpy as np
from jax import lax
from jax.experimental import pallas as pl
from jax.experimental.pallas import tpu as pltpu

D_MODEL = 1024
CHUNK = 64
HEAD_DIM = 64
DIFF_HEADS = 4
SB_HEADS = 8
SLAB = 128
DIFF_WIDTH = DIFF_HEADS * 2 * HEAD_DIM
SB_WIDTH = SB_HEADS * HEAD_DIM
MIX_WIDTH = DIFF_WIDTH + SB_WIDTH
IN_WIDTH = 3 * MIX_WIDTH
D_FF = 4 * D_MODEL
EPS = 1e-6
NEG = -1e30
LAMBDA_INIT = 0.8 - 0.6 * math.exp(-0.3 * 0)
ATTN_SCALE = HEAD_DIM ** -0.5
LOG2E = math.log2(math.e)
Q_SCALE = ATTN_SCALE * LOG2E
ALIBI_SLOPES = tuple(float(2.0 ** (-8.0 * (i + 1) / DIFF_HEADS)) for i in range(DIFF_HEADS))

VMEM_LIMIT = 56 * 1024 * 1024
BF16 = jnp.bfloat16
F32 = jnp.float32

NT_DIMS = (((1,), (1,)), ((), ()))
TN_DIMS = (((0,), (0,)), ((), ()))


def _rms(x, g):
    ms = jnp.mean(x * x, axis=-1, keepdims=True)
    return x * lax.rsqrt(ms + EPS) * g


def _slab(i):
    return slice(i * SLAB, (i + 1) * SLAB)


def _mod_kernel(c_ref, w_ref, b_ref, lq1_ref, lk1_ref, lq2_ref, lk2_ref, m_ref, lam_ref):
    c = c_ref[...]
    s = c * jax.nn.sigmoid(c)
    m_ref[...] = jnp.dot(s, w_ref[...], preferred_element_type=F32,
                         precision=lax.Precision.HIGHEST) + b_ref[...]
    d1 = jnp.sum(lq1_ref[...] * lk1_ref[...], axis=-1, keepdims=True)
    d2 = jnp.sum(lq2_ref[...] * lk2_ref[...], axis=-1, keepdims=True)
    lam = jnp.exp(d1) - jnp.exp(d2) + LAMBDA_INIT
    lam_ref[...] = jnp.broadcast_to(lam, lam_ref.shape)


def _modulation(c_all, w_ada, b_ada, lq1, lk1, lq2, lk2):
    nb = c_all.shape[0]
    tn = 1024
    vec = pl.BlockSpec((1, HEAD_DIM), lambda j: (0, 0))
    return pl.pallas_call(
        _mod_kernel,
        grid=(6 * D_MODEL // tn,),
        in_specs=[pl.BlockSpec((nb, D_MODEL), lambda j: (0, 0)),
                  pl.BlockSpec((D_MODEL, tn), lambda j: (0, j)),
                  pl.BlockSpec((1, tn), lambda j: (0, j)),
                  vec, vec, vec, vec],
        out_specs=[pl.BlockSpec((nb, tn), lambda j: (0, j)),
                   pl.BlockSpec((8, 128), lambda j: (0, 0))],
        out_shape=[jax.ShapeDtypeStruct((nb, 6 * D_MODEL), F32),
                   jax.ShapeDtypeStruct((8, 128), F32)],
        compiler_params=pltpu.CompilerParams(dimension_semantics=("arbitrary",),
                                             vmem_limit_bytes=VMEM_LIMIT),
        name="modulation",
    )(c_all, w_ada, b_ada, lq1, lk1, lq2, lk2)


def _in_kernel(x_ref, mod_ref, g_ref, w_ref,
               q_ref, kb_ref, vb_ref, kd_ref, vd_ref, ks_ref, vs_ref):
    bb, tm, d = x_ref.shape
    x = x_ref[...]
    shift = mod_ref[:, 0:1, :]
    scale = mod_ref[:, 1:2, :]
    h = _rms(x, g_ref[...]) * (1.0 + scale) + shift
    hb = h.reshape(bb * tm, d).astype(BF16)

    def proj(c):
        u = jnp.dot(hb, w_ref[:, c * 512:(c + 1) * 512], preferred_element_type=F32)
        return u.reshape(bb, tm, 512)

    q_ref[:, :, 0:512] = (proj(0) * Q_SCALE).astype(BF16)
    q_ref[:, :, 512:1024] = (proj(3) * Q_SCALE).astype(BF16)
    for c, f_ref, b_ref in ((1, kd_ref, kb_ref), (2, vd_ref, vb_ref)):
        u = proj(c)
        b_ref[:, :, 0:512] = u.astype(BF16)
        for hd in range(DIFF_HEADS):
            f_ref[:, pl.ds(hd, tm, stride=DIFF_HEADS), :] = u[:, :, _slab(hd)]
    for c, f_ref, b_ref in ((4, ks_ref, kb_ref), (5, vs_ref, vb_ref)):
        u = proj(c)
        f_ref[...] = u
        b_ref[:, :, 512:1024] = u.astype(BF16)


def _in_proj(x, mods, mod_off, g, w_bf, bb, tm):
    b, s, d = x.shape
    grid = (b // bb, s // tm)
    tok = lambda w: pl.BlockSpec((bb, tm, w), lambda i, t: (i, t, 0))
    rows = pl.BlockSpec((bb, tm * DIFF_HEADS, SLAB), lambda i, t: (i, t, 0))
    const = lambda shape: pl.BlockSpec(shape, lambda i, t: (0,) * len(shape),
                                       pipeline_mode=pl.Buffered(1))
    outs = [jax.ShapeDtypeStruct((b, s, MIX_WIDTH), BF16)] * 3 + \
           [jax.ShapeDtypeStruct((b, s * DIFF_HEADS, SLAB), F32)] * 2 + \
           [jax.ShapeDtypeStruct((b, s, SB_WIDTH), F32)] * 2
    return pl.pallas_call(
        _in_kernel,
        grid=grid,
        in_specs=[tok(d),
                  pl.BlockSpec((bb, 6, d), lambda i, t: (i + mod_off // bb, 0, 0)),
                  const((1, d)),
                  const((d, IN_WIDTH))],
        out_specs=[tok(MIX_WIDTH)] * 3 + [rows] * 2 + [tok(SB_WIDTH)] * 2,
        out_shape=outs,
        compiler_params=pltpu.CompilerParams(dimension_semantics=("parallel", "parallel"),
                                             vmem_limit_bytes=VMEM_LIMIT),
        name="in_proj",
    )(x, mods, g, w_bf)


def _half_masks(q):
    lane = lax.broadcasted_iota(jnp.int32, q.shape, 1)
    zero = jnp.zeros_like(q)
    return jnp.where(lane < HEAD_DIM, q, zero), jnp.where(lane >= HEAD_DIM, q, zero)


def _staging(refs, slot):
    def sink(idx, *vals):
        for r, v in zip(refs, vals):
            r[slot, idx] = v

    def source(idx, k):
        return refs[k][slot, idx]
    return sink, source


def _list_staging():
    store = {}

    def sink(idx, *vals):
        store[idx] = vals

    def source(idx, k):
        return store[idx][k]
    return sink, source


def _run_pipeline(stages, stage_refs, state, *, n_main, diag_in_main):
    front_mm, front_vpu, back_mm, back_vpu = stages

    def step(i, j_next, src_prev, sc_prev, state):
        sink, _ = _staging(stage_refs, (i + 1) % 2)
        zs = front_mm(j_next)
        pre = back_mm(src_prev, False)
        sc_next = front_vpu(zs, j_next, sink)
        return sc_next, back_vpu(src_prev, pre, sc_prev, state, False)

    def body(i, carry):
        scalars, state = carry
        _, src = _staging(stage_refs, i % 2)
        return step(i, n_main - 1 - i, src, scalars, state)

    if diag_in_main:
        sink, _ = _staging(stage_refs, 0)
        scalars = front_vpu(front_mm(None), None, sink)
        first = 0
    else:
        sink, src = _list_staging()
        scalars = front_vpu(front_mm(None), None, sink)
        state = back_vpu(src, back_mm(src, True), scalars, state, True)
        sink, _ = _staging(stage_refs, 1)
        scalars = front_vpu(front_mm(n_main - 1), n_main - 1, sink)
        first = 1

    scalars, state = lax.fori_loop(first, n_main, body, (scalars, state))
    _, src = _staging(stage_refs, n_main % 2)
    return back_vpu(src, back_mm(src, False), scalars, state, False)


def _diff_kernel(*refs, tk, td, n_main, q_base, diag_base, diag_in_main, stacked, interleaved):
    if diag_in_main:
        lam_ref, q_ref, km_ref, vm_ref, g_ref, o_ref, s_ref, mt_ref = refs
        kd_ref = vd_ref = None
    else:
        lam_ref, q_ref, km_ref, vm_ref, kd_ref, vd_ref, g_ref, o_ref, s_ref, mt_ref = refs
    tq = q_ref.shape[1]
    nmap = 1 if stacked else 2
    nv = 2 * tq if stacked else tq
    qi = pl.program_id(1)
    lam = lam_ref[0]
    q0 = q_base(qi)
    nm = n_main(qi)
    heads = range(DIFF_HEADS)
    slopes = [s * LOG2E for s in ALIBI_SLOPES]
    qs = []
    for h in heads:
        q1, q2 = _half_masks(q_ref[0, :, _slab(h)])
        qs.append([jnp.concatenate([q1, q2], axis=0)] if stacked else [q1, q2])

    def main_block(ref, j, h):
        if interleaved:
            start = pl.multiple_of(j * (tk * DIFF_HEADS), tk * DIFF_HEADS) + h
            return ref[0, pl.ds(start, tk, stride=DIFF_HEADS), :].astype(BF16)
        return ref[0, pl.ds(pl.multiple_of(j * tk, tk), tk), _slab(h)].astype(BF16)

    def qcol(n):
        col = lax.broadcasted_iota(jnp.int32, (n, nv), 1)
        return col & (tq - 1) if stacked else col

    rel = (qcol(tk) - lax.broadcasted_iota(jnp.int32, (tk, nv), 0)).astype(F32)

    def front_mm(j):
        zs = []
        for h in heads:
            if j is not None:
                kblk = main_block(km_ref, j, h)
            elif diag_in_main:
                kblk = main_block(km_ref, qi, h)
            else:
                kblk = kd_ref[0, :, _slab(h)].astype(BF16)
            for qm in qs[h]:
                zs.append(lax.dot_general(kblk, qm, NT_DIMS, preferred_element_type=F32))
        return zs

    def front_vpu(zs, j, sink):
        if j is not None:
            for h in heads:
                bias = -slopes[h] * rel
                for mi in range(nmap):
                    s = zs[nmap * h + mi] + bias
                    sink(nmap * h + mi, s, jnp.max(s, axis=0, keepdims=True))
            return jnp.asarray(q0 - j * tk, F32), jnp.asarray(j, jnp.int32)
        kpos = diag_base(qi) + lax.broadcasted_iota(jnp.int32, (td, nv), 0)
        qpos = q0 + qcol(td)
        visible = (kpos // CHUNK) <= (qpos // CHUNK)
        dist = jnp.abs(qpos - kpos).astype(F32)
        for h in heads:
            bias = -slopes[h] * dist
            for mi in range(nmap):
                s = jnp.where(visible, zs[nmap * h + mi] + bias, NEG)
                sink(nmap * h + mi, s, jnp.max(s, axis=0, keepdims=True))
        return jnp.zeros((), F32), jnp.asarray(qi, jnp.int32)

    def back_mm(source, diag_ref):
        return None

    def back_vpu(source, pre, scalars, state, diag_ref):
        dist0, j = scalars
        new_state = []
        for h in heads:
            vblk = vd_ref[0, :, _slab(h)].astype(BF16) if diag_ref else main_block(vm_ref, j, h)
            off = slopes[h] * dist0
            for mi in range(nmap):
                idx = nmap * h + mi
                m, l, acc = state[idx]
                m_new = jnp.maximum(m, source(idx, 1) - off)
                p = jnp.exp2(source(idx, 0) - (m_new + off))
                alpha = jnp.exp2(m - m_new)
                l_new = alpha * l + jnp.sum(p, axis=0, keepdims=True)
                pv = lax.dot_general(vblk, p.astype(BF16), TN_DIMS, preferred_element_type=F32)
                new_state.append((m_new, l_new, alpha * acc + pv))
        return tuple(new_state)

    init = tuple((jnp.full((1, nv), NEG, F32), jnp.zeros((1, nv), F32), jnp.zeros((SLAB, nv), F32))
                 for _ in range(nmap * DIFF_HEADS))
    state = _run_pipeline((front_mm, front_vpu, back_mm, back_vpu), (s_ref, mt_ref), init,
                          n_main=nm, diag_in_main=diag_in_main)

    for h in heads:
        if stacked:
            _, l, acc = state[h]
            o = acc / l
            o = o[:, 0:tq] - lam * o[:, tq:nv]
        else:
            st1, st2 = state[2 * h], state[2 * h + 1]
            o = st1[2] / st1[1] - lam * (st2[2] / st2[1])
        ms = jnp.mean(o * o, axis=0, keepdims=True)
        o = o * lax.rsqrt(ms + EPS) * g_ref[...] * (1.0 - LAMBDA_INIT)
        o_ref[0, :, _slab(h)] = o.T.astype(o_ref.dtype)


def _softplus2(z):
    return jnp.maximum(z, 0.0) + jnp.log(1.0 + jnp.exp2(-jnp.abs(z))) * LOG2E


def _sb_kernel(q_ref, km_ref, vm_ref, um_ref, o_ref, zt_ref, tb_ref, t0_ref, *, tk):
    tq = q_ref.shape[1]
    qi = pl.program_id(1)
    slabs = range(SB_HEADS // 2)
    qs = [_half_masks(q_ref[0, :, _slab(s)]) for s in slabs]

    def block(ref, j, s):
        return ref[0, pl.ds(pl.multiple_of(j * tk, tk), tk), _slab(s)]

    def front_mm(j):
        zs = []
        for s in slabs:
            kblk = block(km_ref, qi if j is None else j, s)
            for qm in qs[s]:
                zs.append(lax.dot_general(kblk, qm, NT_DIMS, preferred_element_type=F32))
        return zs

    def front_vpu(zs, j, sink):
        earlier = None
        if j is None:
            earlier = (lax.broadcasted_iota(jnp.int32, (tk, tq), 0)
                       < lax.broadcasted_iota(jnp.int32, (tk, tq), 1))
        for idx, z in enumerate(zs):
            t = _softplus2(z)
            zt = z - t
            if earlier is not None:
                t = jnp.where(earlier, t, 0.0)
                zt = jnp.where(earlier, zt, NEG)
            tb = t.astype(BF16)
            sink(idx, zt, tb, tb[0:1, :].astype(F32))
        return (jnp.asarray(qi if j is None else j, jnp.int32),)

    def back_mm(source, diag_ref):
        return [jnp.dot(um_ref[...], source(idx, 1), preferred_element_type=F32)
                for idx in range(SB_HEADS)]

    def back_vpu(source, laters, scalars, state, diag_ref):
        (j,) = scalars
        new_state = []
        for s in slabs:
            vblk = block(vm_ref, j, s)
            for hi in range(2):
                idx = 2 * s + hi
                carry, acc = state[idx]
                later = laters[idx]
                w = jnp.exp2(source(idx, 0) - later - carry)
                pv = lax.dot_general(vblk, w.astype(BF16), TN_DIMS, preferred_element_type=F32)
                new_state.append((carry + later[0:1, :] + source(idx, 2), acc + pv))
        return tuple(new_state)

    init = tuple((jnp.zeros((1, tq), F32), jnp.zeros((SLAB, tq), F32)) for _ in range(SB_HEADS))
    state = _run_pipeline((front_mm, front_vpu, back_mm, back_vpu), (zt_ref, tb_ref, t0_ref), init,
                          n_main=qi, diag_in_main=True)
    row = lax.broadcasted_iota(jnp.int32, (SLAB, tq), 0)
    for s in slabs:
        o = jnp.where(row < HEAD_DIM, state[2 * s][1], state[2 * s + 1][1])
        o_ref[0, :, _slab(s)] = o.T.astype(o_ref.dtype)


def _sb_cache_kernel(q_ref, kt_ref, vt_ref, kn_ref, vn_ref, um_ref, ud_ref, o_ref, *, tk):
    tq = q_ref.shape[1]
    nchunk = kt_ref.shape[2] // tk
    slabs = range(SB_HEADS // 2)
    qv = []
    for s in slabs:
        qa, qb = _half_masks(q_ref[0, :, _slab(s)])
        qv.append(jnp.concatenate([qa, qb], axis=0))

    def chunk(ref, s, j):
        return ref[0, _slab(s), j * tk:(j + 1) * tk].astype(BF16)

    def front(j):
        earlier = None
        if j is None:
            r = lax.broadcasted_iota(jnp.int32, (2 * tq, tq), 0) & (tq - 1)
            earlier = lax.broadcasted_iota(jnp.int32, (2 * tq, tq), 1) < r
        zs = []
        for s in slabs:
            if j is None:
                zs.append(lax.dot_general(qv[s], kn_ref[0, :, _slab(s)], NT_DIMS,
                                          preferred_element_type=F32))
            else:
                zs.append(jnp.dot(qv[s], chunk(kt_ref, s, j), preferred_element_type=F32))
        staged = []
        for z in zs:
            t = _softplus2(z)
            zt = z - t
            if earlier is not None:
                t = jnp.where(earlier, t, 0.0)
                zt = jnp.where(earlier, zt, NEG)
            staged.append((zt, t.astype(BF16)))
        return staged

    def back(staged, j, state):
        u = ud_ref[...] if j is None else um_ref[...]
        tb_all = jnp.concatenate([tb for _, tb in staged], axis=0)
        later_all = jnp.dot(tb_all, u, preferred_element_type=F32)
        new_state = []
        for s in slabs:
            zt, tb = staged[s]
            carry, acc = state[s]
            later = later_all[s * 2 * tq:(s + 1) * 2 * tq]
            w = jnp.exp2(zt - later - carry).astype(BF16)
            if j is None:
                pv = jnp.dot(w, vn_ref[0, :, _slab(s)], preferred_element_type=F32)
            else:
                pv = lax.dot_general(w, chunk(vt_ref, s, j), NT_DIMS, preferred_element_type=F32)
            total = later[:, 0:1] + tb[:, 0:1].astype(F32)
            new_state.append((carry + total, acc + pv))
        return new_state

    state = [(jnp.zeros((2 * tq, 1), F32), jnp.zeros((2 * tq, SLAB), F32)) for _ in slabs]
    order = [None] + list(range(nchunk - 1, -1, -1))
    staged = front(order[0])
    for prev, nxt in zip(order[:-1], order[1:]):
        new = front(nxt)
        state = back(staged, prev, state)
        staged = new
    state = back(staged, order[-1], state)

    lane = lax.broadcasted_iota(jnp.int32, (tq, SLAB), 1)
    for s in slabs:
        acc = state[s][1]
        o_ref[0, :, _slab(s)] = jnp.where(lane < HEAD_DIM, acc[0:tq], acc[tq:2 * tq]).astype(o_ref.dtype)


def _later_keys(n, keys_on_rows):
    r = np.arange(n)
    u = (r[None, :] > r[:, None]) if keys_on_rows else (r[:, None] > r[None, :])
    return jnp.asarray(u.astype(np.float32), dtype=BF16)


def _prompt_attention(q, kb, vb, lam, subln_g, *, tq):
    b, t, _ = q.shape
    grid = (b, t // tq)
    cparams = pltpu.CompilerParams(dimension_semantics=("parallel", "arbitrary"),
                                   vmem_limit_bytes=VMEM_LIMIT)
    smem = pl.BlockSpec(memory_space=pltpu.SMEM)
    g_spec = pl.BlockSpec((SLAB, 1), lambda i, j: (0, 0))
    out_spec = pl.BlockSpec((1, tq, DIFF_WIDTH), lambda i, j: (i, j, 0))
    out_shape = jax.ShapeDtypeStruct((b, t, DIFF_WIDTH), BF16)

    def specs(half):
        return (pl.BlockSpec((1, tq, DIFF_WIDTH), lambda i, j: (i, j, half)),
                pl.BlockSpec((1, t, DIFF_WIDTH), lambda i, j: (i, 0, half)))

    q_spec, kv_spec = specs(0)
    od = pl.pallas_call(
        functools.partial(_diff_kernel, tk=tq, td=tq, n_main=lambda qi: qi,
                          q_base=lambda qi: qi * tq, diag_base=lambda qi: qi * tq,
                          diag_in_main=True, stacked=False, interleaved=False),
        grid=grid, in_specs=[smem, q_spec, kv_spec, kv_spec, g_spec],
        out_specs=out_spec, out_shape=out_shape,
        scratch_shapes=[pltpu.VMEM((2, 2 * DIFF_HEADS, tq, tq), F32),
                        pltpu.VMEM((2, 2 * DIFF_HEADS, 1, tq), F32)],
        compiler_params=cparams, name="diff_attention",
    )(lam, q, kb, vb, subln_g)

    q_spec, kv_spec = specs(1)
    osb = pl.pallas_call(
        functools.partial(_sb_kernel, tk=tq),
        grid=grid,
        in_specs=[q_spec, kv_spec, kv_spec, pl.BlockSpec((tq, tq), lambda i, j: (0, 0))],
        out_specs=out_spec, out_shape=out_shape,
        scratch_shapes=[pltpu.VMEM((2, SB_HEADS, tq, tq), F32),
                        pltpu.VMEM((2, SB_HEADS, tq, tq), BF16),
                        pltpu.VMEM((2, SB_HEADS, 1, tq), F32)],
        compiler_params=cparams, name="sb_attention",
    )(q, kb, vb, _later_keys(tq, True))
    return od, osb


def _sample_attention(q, kb, vb, cdk, cdv, cskt, csvt, lam, subln_g, *, tk):
    b, t, _ = q.shape
    past = cskt.shape[2]
    smem = pl.BlockSpec(memory_space=pltpu.SMEM)
    new = lambda half: pl.BlockSpec((1, t, DIFF_WIDTH), lambda i, j: (i, 0, half))
    out_spec = pl.BlockSpec((1, t, DIFF_WIDTH), lambda i, j: (i, 0, 0))
    out_shape = jax.ShapeDtypeStruct((b, t, DIFF_WIDTH), BF16)
    cparams = pltpu.CompilerParams(dimension_semantics=("parallel", "arbitrary"),
                                   vmem_limit_bytes=VMEM_LIMIT)

    rows = pl.BlockSpec((1, past * DIFF_HEADS, SLAB), lambda i, j: (i, 0, 0))
    od = pl.pallas_call(
        functools.partial(_diff_kernel, tk=tk, td=t, n_main=lambda qi: past // tk,
                          q_base=lambda qi: past, diag_base=lambda qi: past,
                          diag_in_main=False, stacked=True, interleaved=True),
        grid=(b, 1),
        in_specs=[smem, new(0), rows, rows, new(0), new(0),
                  pl.BlockSpec((SLAB, 1), lambda i, j: (0, 0))],
        out_specs=out_spec, out_shape=out_shape,
        scratch_shapes=[pltpu.VMEM((2, DIFF_HEADS, tk, 2 * t), F32),
                        pltpu.VMEM((2, DIFF_HEADS, 1, 2 * t), F32)],
        compiler_params=cparams, name="diff_attention_cache",
    )(lam, q, cdk, cdv, kb, vb, subln_g)

    feat = pl.BlockSpec((1, SB_WIDTH, past), lambda i, j: (i, 0, 0))
    osb = pl.pallas_call(
        functools.partial(_sb_cache_kernel, tk=tk),
        grid=(b, 1),
        in_specs=[new(1), feat, feat, new(1), new(1),
                  pl.BlockSpec((tk, tk), lambda i, j: (0, 0)),
                  pl.BlockSpec((t, t), lambda i, j: (0, 0))],
        out_specs=out_spec, out_shape=out_shape,
        compiler_params=cparams, name="sb_attention_cache",
    )(q, cskt, csvt, kb, vb, _later_keys(tk, False), _later_keys(t, False))
    return od, osb


def _out_kernel(od_ref, osb_ref, x_ref, mod_ref, gpm_ref, gpf_ref, gqf_ref,
                wo_ref, wu_ref, wd_ref, y_ref):
    bb, tm, d = x_ref.shape
    n = bb * tm
    od = od_ref[...].reshape(n, DIFF_WIDTH)
    osb = osb_ref[...].reshape(n, SB_WIDTH)
    y = (jnp.dot(od, wo_ref[0:DIFF_WIDTH, :], preferred_element_type=F32)
         + jnp.dot(osb, wo_ref[DIFF_WIDTH:MIX_WIDTH, :], preferred_element_type=F32))
    y = y.reshape(bb, tm, d)
    gate1 = mod_ref[:, 2:3, :]
    shift2 = mod_ref[:, 3:4, :]
    scale2 = mod_ref[:, 4:5, :]
    gate2 = mod_ref[:, 5:6, :]
    x1 = x_ref[...] + gate1 * _rms(y, gpm_ref[...])
    h2 = _rms(x1, gpf_ref[...]) * (1.0 + scale2) + shift2
    h2b = h2.reshape(n, d).astype(BF16)
    acc = jnp.zeros((n, d), F32)
    fc = 1024
    for c in range(D_FF // fc):
        f = jnp.dot(h2b, wu_ref[:, c * fc:(c + 1) * fc], preferred_element_type=F32)
        r = jnp.square(jnp.maximum(f, 0.0)).astype(BF16)
        acc = acc + jnp.dot(r, wd_ref[c * fc:(c + 1) * fc, :], preferred_element_type=F32)
    y2 = acc.reshape(bb, tm, d)
    y_ref[...] = x1 + gate2 * _rms(y2, gqf_ref[...])


def _out_ffn(od, osb, x, mods, mod_off, g_post_mix, g_pre_ffn, g_post_ffn,
             wo_bf, wu_bf, wd_bf, bb, tm):
    b, s, d = x.shape
    grid = (b // bb, s // tm)
    tok = lambda w: pl.BlockSpec((bb, tm, w), lambda i, t: (i, t, 0))
    const = lambda shape: pl.BlockSpec(shape, lambda i, t: (0,) * len(shape),
                                       pipeline_mode=pl.Buffered(1))
    return pl.pallas_call(
        _out_kernel,
        grid=grid,
        in_specs=[tok(DIFF_WIDTH), tok(SB_WIDTH), tok(d),
                  pl.BlockSpec((bb, 6, d), lambda i, t: (i + mod_off // bb, 0, 0)),
                  const((1, d)), const((1, d)), const((1, d)),
                  const((MIX_WIDTH, d)), const((d, D_FF)), const((D_FF, d))],
        out_specs=tok(d),
        out_shape=jax.ShapeDtypeStruct((b, s, d), F32),
        compiler_params=pltpu.CompilerParams(dimension_semantics=("parallel", "parallel"),
                                             vmem_limit_bytes=VMEM_LIMIT),
        name="out_ffn",
    )(od, osb, x, mods, g_post_mix, g_pre_ffn, g_post_ffn, wo_bf, wu_bf, wd_bf)


def kernel(x_prompt, x_sample, c_prompt, c_sample, cache_diff_k, cache_diff_v, cache_sb_k, cache_sb_v,
           w_ada, b_ada, g_pre_mix, g_post_mix, w_in, lambda_q1, lambda_k1, lambda_q2, lambda_k2,
           diff_subln_g, w_out, g_pre_ffn, g_post_ffn, w_up, w_down):
    bp, sp, d = x_prompt.shape
    bs, ss, _ = x_sample.shape
    past = cache_diff_k.shape[2]
    l = 0

    c_all = jnp.concatenate([c_prompt, c_sample], axis=0)
    mods, lam_tile = _modulation(c_all, w_ada[l], b_ada[l][None, :],
                                 lambda_q1[l][None, :], lambda_k1[l][None, :],
                                 lambda_q2[l][None, :], lambda_k2[l][None, :])
    mods = mods.reshape(bp + bs, 6, d)
    lam = lam_tile[0, 0:1]
    subln_g = diff_subln_g[l].reshape(SLAB, 1)

    w_in_bf = w_in[l].astype(BF16)
    w_out_bf = w_out[l].astype(BF16)
    w_up_bf = w_up[l].astype(BF16)
    w_down_bf = w_down[l].astype(BF16)
    g1 = g_pre_mix[l][None, :]
    g2 = g_post_mix[l][None, :]
    g3 = g_pre_ffn[l][None, :]
    g4 = g_post_ffn[l][None, :]
    diff_shape = lambda b, s: (1, b, s, DIFF_HEADS, 2 * HEAD_DIM)
    sb_shape = lambda b, s: (1, b, s, SB_HEADS, HEAD_DIM)

    q, kb, vb, kd, vd, ks, vs = _in_proj(x_prompt, mods, 0, g1, w_in_bf, bb=1, tm=512)
    od, osb = _prompt_attention(q, kb, vb, lam, subln_g, tq=256)
    y_prompt = _out_ffn(od, osb, x_prompt, mods, 0, g2, g3, g4,
                        w_out_bf, w_up_bf, w_down_bf, bb=1, tm=512)
    prompt_kv = (kd.reshape(diff_shape(bp, sp)), vd.reshape(diff_shape(bp, sp)),
                 ks.reshape(sb_shape(bp, sp)), vs.reshape(sb_shape(bp, sp)))

    q2, kb2, vb2, kd2, vd2, ks2, vs2 = _in_proj(x_sample, mods, bp, g1, w_in_bf, bb=4, tm=ss)
    cdk = cache_diff_k[l].reshape(bs, past * DIFF_HEADS, SLAB)
    cdv = cache_diff_v[l].reshape(bs, past * DIFF_HEADS, SLAB)
    cskt = jnp.transpose(cache_sb_k[l], (0, 2, 3, 1)).reshape(bs, SB_WIDTH, past)
    csvt = jnp.transpose(cache_sb_v[l], (0, 2, 3, 1)).reshape(bs, SB_WIDTH, past)
    od2, osb2 = _sample_attention(q2, kb2, vb2, cdk, cdv, cskt, csvt, lam, subln_g, tk=256)
    y_sample = _out_ffn(od2, osb2, x_sample, mods, bp, g2, g3, g4,
                        w_out_bf, w_up_bf, w_down_bf, bb=4, tm=ss)
    sample_kv = (kd2.reshape(diff_shape(bs, ss)), vd2.reshape(diff_shape(bs, ss)),
                 ks2.reshape(sb_shape(bs, ss)), vs2.reshape(sb_shape(bs, ss)))

    return (y_prompt, y_sample, *prompt_kv, *sample_kv)
```

```python
import functools
import math

import jax
import jax.numpy as jnp
import numpy as np
from jax import lax
from jax.experimental import pallas as pl
from jax.experimental.pallas import tpu as pltpu

D_MODEL = 1024
CHUNK = 64
HEAD_DIM = 64
DIFF_HEADS = 4
SB_HEADS = 8
SLAB = 128
DIFF_WIDTH = DIFF_HEADS * 2 * HEAD_DIM
SB_WIDTH = SB_HEADS * HEAD_DIM
MIX_WIDTH = DIFF_WIDTH + SB_WIDTH
IN_WIDTH = 3 * MIX_WIDTH
D_FF = 4 * D_MODEL
EPS = 1e-6
NEG = -1e30
LAMBDA_INIT = 0.8 - 0.6 * math.exp(-0.3 * 0)
ATTN_SCALE = HEAD_DIM ** -0.5
LOG2E = math.log2(math.e)
Q_SCALE = ATTN_SCALE * LOG2E
ALIBI_SLOPES = tuple(float(2.0 ** (-8.0 * (i + 1) / DIFF_HEADS)) for i in range(DIFF_HEADS))

VMEM_LIMIT = 56 * 1024 * 1024
BF16 = jnp.bfloat16
F32 = jnp.float32

NT_DIMS = (((1,), (1,)), ((), ()))
TN_DIMS = (((0,), (0,)), ((), ()))


def _rms(x, g):
    ms = jnp.mean(x * x, axis=-1, keepdims=True)
    return x * lax.rsqrt(ms + EPS) * g


def _slab(i):
    return slice(i * SLAB, (i + 1) * SLAB)


def _mod_kernel(c_ref, w_ref, b_ref, lq1_ref, lk1_ref, lq2_ref, lk2_ref, m_ref, lam_ref):
    c = c_ref[...]
    s = c * jax.nn.sigmoid(c)
    m_ref[...] = jnp.dot(s, w_ref[...], preferred_element_type=F32,
                         precision=lax.Precision.HIGHEST) + b_ref[...]
    d1 = jnp.sum(lq1_ref[...] * lk1_ref[...], axis=-1, keepdims=True)
    d2 = jnp.sum(lq2_ref[...] * lk2_ref[...], axis=-1, keepdims=True)
    lam = jnp.exp(d1) - jnp.exp(d2) + LAMBDA_INIT
    lam_ref[...] = jnp.broadcast_to(lam, lam_ref.shape)


def _modulation(c_all, w_ada, b_ada, lq1, lk1, lq2, lk2):
    nb = c_all.shape[0]
    tn = 1024
    vec = pl.BlockSpec((1, HEAD_DIM), lambda j: (0, 0))
    return pl.pallas_call(
        _mod_kernel,
        grid=(6 * D_MODEL // tn,),
        in_specs=[pl.BlockSpec((nb, D_MODEL), lambda j: (0, 0)),
                  pl.BlockSpec((D_MODEL, tn), lambda j: (0, j)),
                  pl.BlockSpec((1, tn), lambda j: (0, j)),
                  vec, vec, vec, vec],
        out_specs=[pl.BlockSpec((nb, tn), lambda j: (0, j)),
                   pl.BlockSpec((8, 128), lambda j: (0, 0))],
        out_shape=[jax.ShapeDtypeStruct((nb, 6 * D_MODEL), F32),
                   jax.ShapeDtypeStruct((8, 128), F32)],
        compiler_params=pltpu.CompilerParams(dimension_semantics=("arbitrary",),
                                             vmem_limit_bytes=VMEM_LIMIT),
        name="modulation",
    )(c_all, w_ada, b_ada, lq1, lk1, lq2, lk2)


def _in_kernel(x_ref, mod_ref, g_ref, w_ref,
               q_ref, kb_ref, vb_ref, kd_ref, vd_ref, ks_ref, vs_ref):
    bb, tm, d = x_ref.shape
    x = x_ref[...]
    shift = mod_ref[:, 0:1, :]
    scale = mod_ref[:, 1:2, :]
    h = _rms(x, g_ref[...]) * (1.0 + scale) + shift
    hb = h.reshape(bb * tm, d).astype(BF16)

    def proj(c):
        u = jnp.dot(hb, w_ref[:, c * 512:(c + 1) * 512], preferred_element_type=F32)
        return u.reshape(bb, tm, 512)

    q_ref[:, :, 0:512] = (proj(0) * Q_SCALE).astype(BF16)
    q_ref[:, :, 512:1024] = (proj(3) * Q_SCALE).astype(BF16)
    for c, f_ref, b_ref in ((1, kd_ref, kb_ref), (2, vd_ref, vb_ref)):
        u = proj(c)
        b_ref[:, :, 0:512] = u.astype(BF16)
        for hd in range(DIFF_HEADS):
            f_ref[:, pl.ds(hd, tm, stride=DIFF_HEADS), :] = u[:, :, _slab(hd)]
    for c, f_ref, b_ref in ((4, ks_ref, kb_ref), (5, vs_ref, vb_ref)):
        u = proj(c)
        f_ref[...] = u
        b_ref[:, :, 512:1024] = u.astype(BF16)


def _in_proj(x, mods, mod_off, g, w_bf, bb, tm):
    b, s, d = x.shape
    grid = (b // bb, s // tm)
    tok = lambda w: pl.BlockSpec((bb, tm, w), lambda i, t: (i, t, 0))
    rows = pl.BlockSpec((bb, tm * DIFF_HEADS, SLAB), lambda i, t: (i, t, 0))
    const = lambda shape: pl.BlockSpec(shape, lambda i, t: (0,) * len(shape),
                                       pipeline_mode=pl.Buffered(1))
    outs = [jax.ShapeDtypeStruct((b, s, MIX_WIDTH), BF16)] * 3 + \
           [jax.ShapeDtypeStruct((b, s * DIFF_HEADS, SLAB), F32)] * 2 + \
           [jax.ShapeDtypeStruct((b, s, SB_WIDTH), F32)] * 2
    return pl.pallas_call(
        _in_kernel,
        grid=grid,
        in_specs=[tok(d),
                  pl.BlockSpec((bb, 6, d), lambda i, t: (i + mod_off // bb, 0, 0)),
                  const((1, d)),
                  const((d, IN_WIDTH))],
        out_specs=[tok(MIX_WIDTH)] * 3 + [rows] * 2 + [tok(SB_WIDTH)] * 2,
        out_shape=outs,
        compiler_params=pltpu.CompilerParams(dimension_semantics=("parallel", "parallel"),
                                             vmem_limit_bytes=VMEM_LIMIT),
        name="in_proj",
    )(x, mods, g, w_bf)


def _half_masks(q):
    lane = lax.broadcasted_iota(jnp.int32, q.shape, 1)
    zero = jnp.zeros_like(q)
    return jnp.where(lane < HEAD_DIM, q, zero), jnp.where(lane >= HEAD_DIM, q, zero)


def _staging(refs, slot):
    def sink(idx, *vals):
        for r, v in zip(refs, vals):
            r[slot, idx] = v

    def source(idx, k):
        return refs[k][slot, idx]
    return sink, source


def _list_staging():
    store = {}

    def sink(idx, *vals):
        store[idx] = vals

    def source(idx, k):
        return store[idx][k]
    return sink, source


def _run_pipeline(stages, stage_refs, *, n_main, diag_in_main):
    front_mm, front_vpu, back_mm, back_vpu = stages

    def body(i, scalars):
        sink, _ = _staging(stage_refs, (i + 1) % 2)
        _, src = _staging(stage_refs, i % 2)
        j_next = n_main - 1 - i
        pre = back_mm(src, False)
        zs = front_mm(j_next)
        back_vpu(src, pre, scalars, False)
        return front_vpu(zs, j_next, sink)

    if diag_in_main:
        sink, _ = _staging(stage_refs, 0)
        scalars = front_vpu(front_mm(None), None, sink)
        first = 0
    else:
        sink, src = _list_staging()
        scalars = front_vpu(front_mm(None), None, sink)
        back_vpu(src, back_mm(src, True), scalars, True)
        sink, _ = _staging(stage_refs, 1)
        scalars = front_vpu(front_mm(n_main - 1), n_main - 1, sink)
        first = 1

    scalars = lax.fori_loop(first, n_main, body, scalars)
    _, src = _staging(stage_refs, n_main % 2)
    back_vpu(src, back_mm(src, False), scalars, False)


def _diff_kernel(*refs, tk, td, n_main, q_base, diag_base, diag_in_main, stacked, interleaved):
    if diag_in_main:
        (lam_ref, q_ref, km_ref, vm_ref, g_ref, bm_ref, bd_ref, o_ref,
         s_ref, mt_ref, m_ref, l_ref, acc_ref) = refs
        kd_ref = vd_ref = None
    else:
        (lam_ref, q_ref, km_ref, vm_ref, kd_ref, vd_ref, g_ref, bm_ref, bd_ref, o_ref,
         s_ref, mt_ref, m_ref, l_ref, acc_ref) = refs
    tq = q_ref.shape[1]
    nmap = 1 if stacked else 2
    nv = 2 * tq if stacked else tq
    qi = pl.program_id(1)
    lam = lam_ref[0]
    q0 = q_base(qi)
    nm = n_main(qi)
    heads = range(DIFF_HEADS)
    slopes = [s * LOG2E for s in ALIBI_SLOPES]
    qs = []
    for h in heads:
        q1, q2 = _half_masks(q_ref[0, :, _slab(h)])
        qs.append([jnp.concatenate([q1, q2], axis=0)] if stacked else [q1, q2])

    def main_block(ref, j, h):
        if interleaved:
            start = pl.multiple_of(j * (tk * DIFF_HEADS), tk * DIFF_HEADS) + h
            return ref[0, pl.ds(start, tk, stride=DIFF_HEADS), :].astype(BF16)
        return ref[0, pl.ds(pl.multiple_of(j * tk, tk), tk), _slab(h)].astype(BF16)

    def front_mm(j):
        zs = []
        for h in heads:
            if j is not None:
                kblk = main_block(km_ref, j, h)
            elif diag_in_main:
                kblk = main_block(km_ref, qi, h)
            else:
                kblk = kd_ref[0, :, _slab(h)].astype(BF16)
            for qm in qs[h]:
                zs.append(lax.dot_general(kblk, qm, NT_DIMS, preferred_element_type=F32))
        return zs

    def front_vpu(zs, j, sink):
        bias_ref = bd_ref if j is None else bm_ref
        for h in heads:
            for mi in range(nmap):
                s = zs[nmap * h + mi] + bias_ref[h]
                sink(nmap * h + mi, s, jnp.max(s, axis=0, keepdims=True))
        if j is None:
            return jnp.zeros((), F32), jnp.asarray(qi, jnp.int32)
        return jnp.asarray(q0 - j * tk, F32), jnp.asarray(j, jnp.int32)

    def back_mm(source, diag_ref):
        return None

    def back_vpu(source, pre, scalars, diag_ref):
        dist0, j = scalars
        for h in heads:
            vblk = vd_ref[0, :, _slab(h)].astype(BF16) if diag_ref else main_block(vm_ref, j, h)
            off = slopes[h] * dist0
            for mi in range(nmap):
                idx = nmap * h + mi
                m = m_ref[idx]
                m_new = jnp.maximum(m, source(idx, 1) - off)
                p = jnp.exp2(source(idx, 0) - (m_new + off))
                alpha = jnp.exp2(m - m_new)
                m_ref[idx] = m_new
                l_ref[idx] = alpha * l_ref[idx] + jnp.sum(p, axis=0, keepdims=True)
                pv = lax.dot_general(vblk, p.astype(BF16), TN_DIMS, preferred_element_type=F32)
                acc_ref[idx] = alpha * acc_ref[idx] + pv

    m_ref[...] = jnp.full(m_ref.shape, NEG, F32)
    l_ref[...] = jnp.zeros(l_ref.shape, F32)
    acc_ref[...] = jnp.zeros(acc_ref.shape, F32)
    _run_pipeline((front_mm, front_vpu, back_mm, back_vpu), (s_ref, mt_ref),
                  n_main=nm, diag_in_main=diag_in_main)

    for h in heads:
        if stacked:
            o = acc_ref[h] * (1.0 / l_ref[h])
            o = o[:, 0:tq] - lam * o[:, tq:nv]
        else:
            o = (acc_ref[2 * h] * (1.0 / l_ref[2 * h])
                 - acc_ref[2 * h + 1] * (lam / l_ref[2 * h + 1]))
        ms = jnp.mean(o * o, axis=0, keepdims=True)
        o = o * lax.rsqrt(ms + EPS) * g_ref[...] * (1.0 - LAMBDA_INIT)
        o_ref[0, :, _slab(h)] = o.T.astype(o_ref.dtype)


def _softplus2(z):
    return jnp.maximum(z, 0.0) + jnp.log(1.0 + jnp.exp2(-jnp.abs(z))) * LOG2E


def _sb_kernel(q_ref, km_ref, vm_ref, um_ref, o_ref, zt_ref, tb_ref, t0_ref, carry_ref, acc_ref,
               *, tk):
    tq = q_ref.shape[1]
    qi = pl.program_id(1)
    slabs = range(SB_HEADS // 2)
    qs = [_half_masks(q_ref[0, :, _slab(s)]) for s in slabs]

    def block(ref, j, s):
        return ref[0, pl.ds(pl.multiple_of(j * tk, tk), tk), _slab(s)]

    def front_mm(j):
        zs = []
        for s in slabs:
            kblk = block(km_ref, qi if j is None else j, s)
            for qm in qs[s]:
                zs.append(lax.dot_general(kblk, qm, NT_DIMS, preferred_element_type=F32))
        return zs

    def front_vpu(zs, j, sink):
        earlier = None
        if j is None:
            earlier = (lax.broadcasted_iota(jnp.int32, (tk, tq), 0)
                       < lax.broadcasted_iota(jnp.int32, (tk, tq), 1))
        for idx, z in enumerate(zs):
            t = _softplus2(z)
            zt = z - t
            if earlier is not None:
                t = jnp.where(earlier, t, 0.0)
                zt = jnp.where(earlier, zt, NEG)
            tb = t.astype(BF16)
            sink(idx, zt, tb, tb[0:1, :].astype(F32))
        return (jnp.asarray(qi if j is None else j, jnp.int32),)

    def back_mm(source, diag_ref):
        return [jnp.dot(um_ref[...], source(idx, 1), preferred_element_type=F32)
                for idx in range(SB_HEADS)]

    def back_vpu(source, laters, scalars, diag_ref):
        (j,) = scalars
        for s in slabs:
            vblk = block(vm_ref, j, s)
            for hi in range(2):
                idx = 2 * s + hi
                carry = carry_ref[idx]
                later = laters[idx]
                w = jnp.exp2(source(idx, 0) - later - carry)
                pv = lax.dot_general(vblk, w.astype(BF16), TN_DIMS, preferred_element_type=F32)
                carry_ref[idx] = carry + later[0:1, :] + source(idx, 2)
                acc_ref[idx] += pv

    carry_ref[...] = jnp.zeros(carry_ref.shape, F32)
    acc_ref[...] = jnp.zeros(acc_ref.shape, F32)
    _run_pipeline((front_mm, front_vpu, back_mm, back_vpu), (zt_ref, tb_ref, t0_ref),
                  n_main=qi, diag_in_main=True)
    row = lax.broadcasted_iota(jnp.int32, (SLAB, tq), 0)
    for s in slabs:
        o = jnp.where(row < HEAD_DIM, acc_ref[2 * s], acc_ref[2 * s + 1])
        o_ref[0, :, _slab(s)] = o.T.astype(o_ref.dtype)


def _sb_cache_kernel(q_ref, kt_ref, vt_ref, kn_ref, vn_ref, um_ref, ud_ref, o_ref, *, tk):
    tq = q_ref.shape[1]
    nchunk = kt_ref.shape[2] // tk
    slabs = range(SB_HEADS // 2)
    qv = []
    for s in slabs:
        qa, qb = _half_masks(q_ref[0, :, _slab(s)])
        qv.append(jnp.concatenate([qa, qb], axis=0))

    def chunk(ref, s, j):
        return ref[0, _slab(s), j * tk:(j + 1) * tk].astype(BF16)

    def front(j):
        earlier = None
        if j is None:
            r = lax.broadcasted_iota(jnp.int32, (2 * tq, tq), 0) & (tq - 1)
            earlier = lax.broadcasted_iota(jnp.int32, (2 * tq, tq), 1) < r
        zs = []
        for s in slabs:
            if j is None:
                zs.append(lax.dot_general(qv[s], kn_ref[0, :, _slab(s)], NT_DIMS,
                                          preferred_element_type=F32))
            else:
                zs.append(jnp.dot(qv[s], chunk(kt_ref, s, j), preferred_element_type=F32))
        staged = []
        for z in zs:
            t = _softplus2(z)
            zt = z - t
            if earlier is not None:
                t = jnp.where(earlier, t, 0.0)
                zt = jnp.where(earlier, zt, NEG)
            staged.append((zt, t.astype(BF16)))
        return staged

    def back(staged, j, state):
        u = ud_ref[...] if j is None else um_ref[...]
        tb_all = jnp.concatenate([tb for _, tb in staged], axis=0)
        later_all = jnp.dot(tb_all, u, preferred_element_type=F32)
        new_state = []
        for s in slabs:
            zt, tb = staged[s]
            carry, acc = state[s]
            later = later_all[s * 2 * tq:(s + 1) * 2 * tq]
            w = jnp.exp2(zt - later - carry).astype(BF16)
            if j is None:
                pv = jnp.dot(w, vn_ref[0, :, _slab(s)], preferred_element_type=F32)
            else:
                pv = lax.dot_general(w, chunk(vt_ref, s, j), NT_DIMS, preferred_element_type=F32)
            total = later[:, 0:1] + tb[:, 0:1].astype(F32)
            new_state.append((carry + total, acc + pv))
        return new_state

    state = [(jnp.zeros((2 * tq, 1), F32), jnp.zeros((2 * tq, SLAB), F32)) for _ in slabs]
    order = [None] + list(range(nchunk - 1, -1, -1))
    staged = front(order[0])
    for prev, nxt in zip(order[:-1], order[1:]):
        new = front(nxt)
        state = back(staged, prev, state)
        staged = new
    state = back(staged, order[-1], state)

    lane = lax.broadcasted_iota(jnp.int32, (tq, SLAB), 1)
    for s in slabs:
        acc = state[s][1]
        o_ref[0, :, _slab(s)] = jnp.where(lane < HEAD_DIM, acc[0:tq], acc[tq:2 * tq]).astype(o_ref.dtype)


def _later_keys(n, keys_on_rows):
    r = np.arange(n)
    u = (r[None, :] > r[:, None]) if keys_on_rows else (r[:, None] > r[None, :])
    return jnp.asarray(u.astype(np.float32), dtype=BF16)


def _alibi_tables(tq, tk, td, stacked):
    nv = 2 * tq if stacked else tq
    qc = np.arange(nv) % tq
    slopes = np.asarray(ALIBI_SLOPES, np.float64)[:, None, None] * LOG2E
    main = -slopes * (qc[None, :] - np.arange(tk)[:, None])[None]
    r = np.arange(td)[:, None]
    diag = np.where((r // CHUNK) <= (qc[None, :] // CHUNK), -slopes * np.abs(qc[None, :] - r)[None], NEG)
    return jnp.asarray(main, F32), jnp.asarray(diag, F32)


def _prompt_attention(q, kb, vb, lam, subln_g, *, tq):
    b, t, _ = q.shape
    grid = (b, t // tq)
    cparams = pltpu.CompilerParams(dimension_semantics=("parallel", "arbitrary"),
                                   vmem_limit_bytes=VMEM_LIMIT)
    smem = pl.BlockSpec(memory_space=pltpu.SMEM)
    g_spec = pl.BlockSpec((SLAB, 1), lambda i, j: (0, 0))
    out_spec = pl.BlockSpec((1, tq, DIFF_WIDTH), lambda i, j: (i, j, 0))
    out_shape = jax.ShapeDtypeStruct((b, t, DIFF_WIDTH), BF16)

    def specs(half):
        return (pl.BlockSpec((1, tq, DIFF_WIDTH), lambda i, j: (i, j, half)),
                pl.BlockSpec((1, t, DIFF_WIDTH), lambda i, j: (i, 0, half)))

    q_spec, kv_spec = specs(0)
    tab_spec = pl.BlockSpec((DIFF_HEADS, tq, tq), lambda i, j: (0, 0, 0))
    od = pl.pallas_call(
        functools.partial(_diff_kernel, tk=tq, td=tq, n_main=lambda qi: qi,
                          q_base=lambda qi: qi * tq, diag_base=lambda qi: qi * tq,
                          diag_in_main=True, stacked=False, interleaved=False),
        grid=grid, in_specs=[smem, q_spec, kv_spec, kv_spec, g_spec, tab_spec, tab_spec],
        out_specs=out_spec, out_shape=out_shape,
        scratch_shapes=[pltpu.VMEM((2, 2 * DIFF_HEADS, tq, tq), F32),
                        pltpu.VMEM((2, 2 * DIFF_HEADS, 1, tq), F32),
                        pltpu.VMEM((2 * DIFF_HEADS, 1, tq), F32),
                        pltpu.VMEM((2 * DIFF_HEADS, 1, tq), F32),
                        pltpu.VMEM((2 * DIFF_HEADS, SLAB, tq), F32)],
        compiler_params=cparams, name="diff_attention",
    )(lam, q, kb, vb, subln_g, *_alibi_tables(tq, tq, tq, False))

    q_spec, kv_spec = specs(1)
    osb = pl.pallas_call(
        functools.partial(_sb_kernel, tk=tq),
        grid=grid,
        in_specs=[q_spec, kv_spec, kv_spec, pl.BlockSpec((tq, tq), lambda i, j: (0, 0))],
        out_specs=out_spec, out_shape=out_shape,
        scratch_shapes=[pltpu.VMEM((2, SB_HEADS, tq, tq), F32),
                        pltpu.VMEM((2, SB_HEADS, tq, tq), BF16),
                        pltpu.VMEM((2, SB_HEADS, 1, tq), F32),
                        pltpu.VMEM((SB_HEADS, 1, tq), F32),
                        pltpu.VMEM((SB_HEADS, SLAB, tq), F32)],
        compiler_params=cparams, name="sb_attention",
    )(q, kb, vb, _later_keys(tq, True))
    return od, osb


def _sample_attention(q, kb, vb, cdk, cdv, cskt, csvt, lam, subln_g, *, tk):
    b, t, _ = q.shape
    past = cskt.shape[2]
    smem = pl.BlockSpec(memory_space=pltpu.SMEM)
    new = lambda half: pl.BlockSpec((1, t, DIFF_WIDTH), lambda i, j: (i, 0, half))
    out_spec = pl.BlockSpec((1, t, DIFF_WIDTH), lambda i, j: (i, 0, 0))
    out_shape = jax.ShapeDtypeStruct((b, t, DIFF_WIDTH), BF16)
    cparams = pltpu.CompilerParams(dimension_semantics=("parallel", "arbitrary"),
                                   vmem_limit_bytes=VMEM_LIMIT)

    rows = pl.BlockSpec((1, past * DIFF_HEADS, SLAB), lambda i, j: (i, 0, 0))
    od = pl.pallas_call(
        functools.partial(_diff_kernel, tk=tk, td=t, n_main=lambda qi: past // tk,
                          q_base=lambda qi: past, diag_base=lambda qi: past,
                          diag_in_main=False, stacked=True, interleaved=True),
        grid=(b, 1),
        in_specs=[smem, new(0), rows, rows, new(0), new(0),
                  pl.BlockSpec((SLAB, 1), lambda i, j: (0, 0)),
                  pl.BlockSpec((DIFF_HEADS, tk, 2 * t), lambda i, j: (0, 0, 0)),
                  pl.BlockSpec((DIFF_HEADS, t, 2 * t), lambda i, j: (0, 0, 0))],
        out_specs=out_spec, out_shape=out_shape,
        scratch_shapes=[pltpu.VMEM((2, DIFF_HEADS, tk, 2 * t), F32),
                        pltpu.VMEM((2, DIFF_HEADS, 1, 2 * t), F32),
                        pltpu.VMEM((DIFF_HEADS, 1, 2 * t), F32),
                        pltpu.VMEM((DIFF_HEADS, 1, 2 * t), F32),
                        pltpu.VMEM((DIFF_HEADS, SLAB, 2 * t), F32)],
        compiler_params=cparams, name="diff_attention_cache",
    )(lam, q, cdk, cdv, kb, vb, subln_g, *_alibi_tables(t, tk, t, True))

    feat = pl.BlockSpec((1, SB_WIDTH, past), lambda i, j: (i, 0, 0))
    osb = pl.pallas_call(
        functools.partial(_sb_cache_kernel, tk=tk),
        grid=(b, 1),
        in_specs=[new(1), feat, feat, new(1), new(1),
                  pl.BlockSpec((tk, tk), lambda i, j: (0, 0)),
                  pl.BlockSpec((t, t), lambda i, j: (0, 0))],
        out_specs=out_spec, out_shape=out_shape,
        compiler_params=cparams, name="sb_attention_cache",
    )(q, cskt, csvt, kb, vb, _later_keys(tk, False), _later_keys(t, False))
    return od, osb


def _out_kernel(od_ref, osb_ref, x_ref, mod_ref, gpm_ref, gpf_ref, gqf_ref,
                wo_ref, wu_ref, wd_ref, y_ref):
    bb, tm, d = x_ref.shape
    n = bb * tm
    od = od_ref[...].reshape(n, DIFF_WIDTH)
    osb = osb_ref[...].reshape(n, SB_WIDTH)
    y = (jnp.dot(od, wo_ref[0:DIFF_WIDTH, :], preferred_element_type=F32)
         + jnp.dot(osb, wo_ref[DIFF_WIDTH:MIX_WIDTH, :], preferred_element_type=F32))
    y = y.reshape(bb, tm, d)
    gate1 = mod_ref[:, 2:3, :]
    shift2 = mod_ref[:, 3:4, :]
    scale2 = mod_ref[:, 4:5, :]
    gate2 = mod_ref[:, 5:6, :]
    x1 = x_ref[...] + gate1 * _rms(y, gpm_ref[...])
    h2 = _rms(x1, gpf_ref[...]) * (1.0 + scale2) + shift2
    h2b = h2.reshape(n, d).astype(BF16)
    acc = jnp.zeros((n, d), F32)
    fc = 1024
    for c in range(D_FF // fc):
        f = jnp.dot(h2b, wu_ref[:, c * fc:(c + 1) * fc], preferred_element_type=F32)
        r = jnp.square(jnp.maximum(f, 0.0)).astype(BF16)
        acc = acc + jnp.dot(r, wd_ref[c * fc:(c + 1) * fc, :], preferred_element_type=F32)
    y2 = acc.reshape(bb, tm, d)
    y_ref[...] = x1 + gate2 * _rms(y2, gqf_ref[...])


def _out_ffn(od, osb, x, mods, mod_off, g_post_mix, g_pre_ffn, g_post_ffn,
             wo_bf, wu_bf, wd_bf, bb, tm):
    b, s, d = x.shape
    grid = (b // bb, s // tm)
    tok = lambda w: pl.BlockSpec((bb, tm, w), lambda i, t: (i, t, 0))
    const = lambda shape: pl.BlockSpec(shape, lambda i, t: (0,) * len(shape),
                                       pipeline_mode=pl.Buffered(1))
    return pl.pallas_call(
        _out_kernel,
        grid=grid,
        in_specs=[tok(DIFF_WIDTH), tok(SB_WIDTH), tok(d),
                  pl.BlockSpec((bb, 6, d), lambda i, t: (i + mod_off // bb, 0, 0)),
                  const((1, d)), const((1, d)), const((1, d)),
                  const((MIX_WIDTH, d)), const((d, D_FF)), const((D_FF, d))],
        out_specs=tok(d),
        out_shape=jax.ShapeDtypeStruct((b, s, d), F32),
        compiler_params=pltpu.CompilerParams(dimension_semantics=("parallel", "parallel"),
                                             vmem_limit_bytes=VMEM_LIMIT),
        name="out_ffn",
    )(od, osb, x, mods, g_post_mix, g_pre_ffn, g_post_ffn, wo_bf, wu_bf, wd_bf)


def kernel(x_prompt, x_sample, c_prompt, c_sample, cache_diff_k, cache_diff_v, cache_sb_k, cache_sb_v,
           w_ada, b_ada, g_pre_mix, g_post_mix, w_in, lambda_q1, lambda_k1, lambda_q2, lambda_k2,
           diff_subln_g, w_out, g_pre_ffn, g_post_ffn, w_up, w_down):
    bp, sp, d = x_prompt.shape
    bs, ss, _ = x_sample.shape
    past = cache_diff_k.shape[2]
    l = 0

    c_all = jnp.concatenate([c_prompt, c_sample], axis=0)
    mods, lam_tile = _modulation(c_all, w_ada[l], b_ada[l][None, :],
                                 lambda_q1[l][None, :], lambda_k1[l][None, :],
                                 lambda_q2[l][None, :], lambda_k2[l][None, :])
    mods = mods.reshape(bp + bs, 6, d)
    lam = lam_tile[0, 0:1]
    subln_g = diff_subln_g[l].reshape(SLAB, 1)

    w_in_bf = w_in[l].astype(BF16)
    w_out_bf = w_out[l].astype(BF16)
    w_up_bf = w_up[l].astype(BF16)
    w_down_bf = w_down[l].astype(BF16)
    g1 = g_pre_mix[l][None, :]
    g2 = g_post_mix[l][None, :]
    g3 = g_pre_ffn[l][None, :]
    g4 = g_post_ffn[l][None, :]
    diff_shape = lambda b, s: (1, b, s, DIFF_HEADS, 2 * HEAD_DIM)
    sb_shape = lambda b, s: (1, b, s, SB_HEADS, HEAD_DIM)

    q, kb, vb, kd, vd, ks, vs = _in_proj(x_prompt, mods, 0, g1, w_in_bf, bb=1, tm=512)
    od, osb = _prompt_attention(q, kb, vb, lam, subln_g, tq=256)
    y_prompt = _out_ffn(od, osb, x_prompt, mods, 0, g2, g3, g4,
                        w_out_bf, w_up_bf, w_down_bf, bb=1, tm=512)
    prompt_kv = (kd.reshape(diff_shape(bp, sp)), vd.reshape(diff_shape(bp, sp)),
                 ks.reshape(sb_shape(bp, sp)), vs.reshape(sb_shape(bp, sp)))

    q2, kb2, vb2, kd2, vd2, ks2, vs2 = _in_proj(x_sample, mods, bp, g1, w_in_bf, bb=8, tm=ss)
    cdk = cache_diff_k[l].reshape(bs, past * DIFF_HEADS, SLAB)
    cdv = cache_diff_v[l].reshape(bs, past * DIFF_HEADS, SLAB)
    cskt = jnp.transpose(cache_sb_k[l], (0, 2, 3, 1)).reshape(bs, SB_WIDTH, past)
    csvt = jnp.transpose(cache_sb_v[l], (0, 2, 3, 1)).reshape(bs, SB_WIDTH, past)
    od2, osb2 = _sample_attention(q2, kb2, vb2, cdk, cdv, cskt, csvt, lam, subln_g, tk=256)
    y_sample = _out_ffn(od2, osb2, x_sample, mods, bp, g2, g3, g4,
                        w_out_bf, w_up_bf, w_down_bf, bb=8, tm=ss)
    sample_kv = (kd2.reshape(diff_shape(bs, ss)), vd2.reshape(diff_shape(bs, ss)),
                 ks2.reshape(sb_shape(bs, ss)), vs2.reshape(sb_shape(bs, ss)))

    return (y_prompt, y_sample, *prompt_kv, *sample_kv)
```

```python
import functools
import math

import jax
import jax.numpy as jnp
import numpy as np
from jax import lax
from jax.experimental import pallas as pl
from jax.experimental.pallas import tpu as pltpu

D_MODEL = 1024
CHUNK = 64
HEAD_DIM = 64
DIFF_HEADS = 4
SB_HEADS = 8
SLAB = 128
DIFF_WIDTH = DIFF_HEADS * 2 * HEAD_DIM
SB_WIDTH = SB_HEADS * HEAD_DIM
MIX_WIDTH = DIFF_WIDTH + SB_WIDTH
IN_WIDTH = 3 * MIX_WIDTH
D_FF = 4 * D_MODEL
EPS = 1e-6
NEG = -1e30
LAMBDA_INIT = 0.8 - 0.6 * math.exp(-0.3 * 0)
ATTN_SCALE = HEAD_DIM ** -0.5
LOG2E = math.log2(math.e)
Q_SCALE = ATTN_SCALE * LOG2E
ALIBI_SLOPES = tuple(float(2.0 ** (-8.0 * (i + 1) / DIFF_HEADS)) for i in range(DIFF_HEADS))

VMEM_LIMIT = 56 * 1024 * 1024
BF16 = jnp.bfloat16
F32 = jnp.float32

NT_DIMS = (((1,), (1,)), ((), ()))
TN_DIMS = (((0,), (0,)), ((), ()))


def _rms(x, g):
    ms = jnp.mean(x * x, axis=-1, keepdims=True)
    return x * lax.rsqrt(ms + EPS) * g


def _slab(i):
    return slice(i * SLAB, (i + 1) * SLAB)


def _mod_kernel(c_ref, w_ref, b_ref, lq1_ref, lk1_ref, lq2_ref, lk2_ref, m_ref, lam_ref):
    c = c_ref[...]
    s = c * jax.nn.sigmoid(c)
    m_ref[...] = jnp.dot(s, w_ref[...], preferred_element_type=F32,
                         precision=lax.Precision.HIGHEST) + b_ref[...]
    d1 = jnp.sum(lq1_ref[...] * lk1_ref[...], axis=-1, keepdims=True)
    d2 = jnp.sum(lq2_ref[...] * lk2_ref[...], axis=-1, keepdims=True)
    lam = jnp.exp(d1) - jnp.exp(d2) + LAMBDA_INIT
    lam_ref[...] = jnp.broadcast_to(lam, lam_ref.shape)


def _modulation(c_all, w_ada, b_ada, lq1, lk1, lq2, lk2):
    nb = c_all.shape[0]
    tn = 1024
    vec = pl.BlockSpec((1, HEAD_DIM), lambda j: (0, 0))
    return pl.pallas_call(
        _mod_kernel,
        grid=(6 * D_MODEL // tn,),
        in_specs=[pl.BlockSpec((nb, D_MODEL), lambda j: (0, 0)),
                  pl.BlockSpec((D_MODEL, tn), lambda j: (0, j)),
                  pl.BlockSpec((1, tn), lambda j: (0, j)),
                  vec, vec, vec, vec],
        out_specs=[pl.BlockSpec((nb, tn), lambda j: (0, j)),
                   pl.BlockSpec((8, 128), lambda j: (0, 0))],
        out_shape=[jax.ShapeDtypeStruct((nb, 6 * D_MODEL), F32),
                   jax.ShapeDtypeStruct((8, 128), F32)],
        compiler_params=pltpu.CompilerParams(dimension_semantics=("arbitrary",),
                                             vmem_limit_bytes=VMEM_LIMIT),
        name="modulation",
    )(c_all, w_ada, b_ada, lq1, lk1, lq2, lk2)


def _in_kernel(x_ref, mod_ref, g_ref, w_ref,
               q_ref, kb_ref, vb_ref, kd_ref, vd_ref, ks_ref, vs_ref, *, sb_feature_major):
    bb, tm, d = x_ref.shape
    x = x_ref[...]
    shift = mod_ref[:, 0:1, :]
    scale = mod_ref[:, 1:2, :]
    h = _rms(x, g_ref[...]) * (1.0 + scale) + shift
    hb = h.reshape(bb * tm, d).astype(BF16)

    def proj(c):
        u = jnp.dot(hb, w_ref[:, c * 512:(c + 1) * 512], preferred_element_type=F32)
        return u.reshape(bb, tm, 512)

    q_ref[:, :, 0:512] = (proj(0) * Q_SCALE).astype(BF16)
    q_ref[:, :, 512:1024] = (proj(3) * Q_SCALE).astype(BF16)
    for c, f_ref, b_ref in ((1, kd_ref, kb_ref), (2, vd_ref, vb_ref)):
        u = proj(c)
        b_ref[:, :, 0:512] = u.astype(BF16)
        for hd in range(DIFF_HEADS):
            f_ref[:, pl.ds(hd, tm, stride=DIFF_HEADS), :] = u[:, :, _slab(hd)]
    for c, f_ref, b_ref in ((4, ks_ref, kb_ref), (5, vs_ref, vb_ref)):
        u = proj(c)
        b_ref[:, :, 512:1024] = u.astype(BF16)
        if sb_feature_major:
            for i in range(bb):
                f_ref[i] = u[i].T
        else:
            f_ref[...] = u


def _in_proj(x, mods, mod_off, g, w_bf, bb, tm, sb_feature_major):
    b, s, d = x.shape
    grid = (b // bb, s // tm)
    tok = lambda w: pl.BlockSpec((bb, tm, w), lambda i, t: (i, t, 0))
    rows = pl.BlockSpec((bb, tm * DIFF_HEADS, SLAB), lambda i, t: (i, t, 0))
    const = lambda shape: pl.BlockSpec(shape, lambda i, t: (0,) * len(shape),
                                       pipeline_mode=pl.Buffered(1))
    outs = [jax.ShapeDtypeStruct((b, s, MIX_WIDTH), BF16)] * 3 + \
           [jax.ShapeDtypeStruct((b, s * DIFF_HEADS, SLAB), F32)] * 2 + \
           [jax.ShapeDtypeStruct((b, SB_WIDTH, s) if sb_feature_major else (b, s, SB_WIDTH), F32)] * 2
    sb_spec = (pl.BlockSpec((bb, SB_WIDTH, tm), lambda i, t: (i, 0, t)) if sb_feature_major
               else tok(SB_WIDTH))
    return pl.pallas_call(
        functools.partial(_in_kernel, sb_feature_major=sb_feature_major),
        grid=grid,
        in_specs=[tok(d),
                  pl.BlockSpec((bb, 6, d), lambda i, t: (i + mod_off // bb, 0, 0)),
                  const((1, d)),
                  const((d, IN_WIDTH))],
        out_specs=[tok(MIX_WIDTH)] * 3 + [rows] * 2 + [sb_spec] * 2,
        out_shape=outs,
        compiler_params=pltpu.CompilerParams(dimension_semantics=("parallel", "parallel"),
                                             vmem_limit_bytes=VMEM_LIMIT),
        name="in_proj",
    )(x, mods, g, w_bf)


def _half_masks(q):
    lane = lax.broadcasted_iota(jnp.int32, q.shape, 1)
    zero = jnp.zeros_like(q)
    return jnp.where(lane < HEAD_DIM, q, zero), jnp.where(lane >= HEAD_DIM, q, zero)


def _staging(refs, slot):
    def sink(idx, *vals):
        for r, v in zip(refs, vals):
            r[slot, idx] = v

    def source(idx, k):
        return refs[k][slot, idx]
    return sink, source


def _list_staging():
    store = {}

    def sink(idx, *vals):
        store[idx] = vals

    def source(idx, k):
        return store[idx][k]
    return sink, source


def _run_pipeline(stages, stage_refs, *, n_main, diag_in_main):
    front_mm, front_vpu, back_mm, back_vpu = stages

    def body(i, scalars):
        sink, _ = _staging(stage_refs, (i + 1) % 2)
        _, src = _staging(stage_refs, i % 2)
        j_next = n_main - 1 - i
        pre = back_mm(src, False)
        zs = front_mm(j_next)
        back_vpu(src, pre, scalars, False)
        return front_vpu(zs, j_next, sink)

    if diag_in_main:
        sink, _ = _staging(stage_refs, 0)
        scalars = front_vpu(front_mm(None), None, sink)
        first = 0
    else:
        sink, src = _list_staging()
        scalars = front_vpu(front_mm(None), None, sink)
        back_vpu(src, back_mm(src, True), scalars, True)
        sink, _ = _staging(stage_refs, 1)
        scalars = front_vpu(front_mm(n_main - 1), n_main - 1, sink)
        first = 1

    scalars = lax.fori_loop(first, n_main, body, scalars)
    _, src = _staging(stage_refs, n_main % 2)
    back_vpu(src, back_mm(src, False), scalars, False)


def _diff_kernel(*refs, tk, td, n_main, q_base, diag_base, diag_in_main, stacked, interleaved):
    if diag_in_main:
        (lam_ref, q_ref, km_ref, vm_ref, g_ref, bm_ref, bd_ref, o_ref,
         s_ref, mt_ref, m_ref, l_ref, acc_ref) = refs
        kd_ref = vd_ref = None
    else:
        (lam_ref, q_ref, km_ref, vm_ref, kd_ref, vd_ref, g_ref, bm_ref, bd_ref, o_ref,
         s_ref, mt_ref, m_ref, l_ref, acc_ref) = refs
    tq = q_ref.shape[1]
    nmap = 1 if stacked else 2
    nv = 2 * tq if stacked else tq
    qi = pl.program_id(1)
    lam = lam_ref[0]
    q0 = q_base(qi)
    nm = n_main(qi)
    heads = range(DIFF_HEADS)
    slopes = [s * LOG2E for s in ALIBI_SLOPES]
    qs = []
    for h in heads:
        q1, q2 = _half_masks(q_ref[0, :, _slab(h)])
        qs.append([jnp.concatenate([q1, q2], axis=0)] if stacked else [q1, q2])

    def main_block(ref, j, h):
        if interleaved:
            start = pl.multiple_of(j * (tk * DIFF_HEADS), tk * DIFF_HEADS) + h
            return ref[0, pl.ds(start, tk, stride=DIFF_HEADS), :].astype(BF16)
        return ref[0, pl.ds(pl.multiple_of(j * tk, tk), tk), _slab(h)].astype(BF16)

    def front_mm(j):
        zs = []
        for h in heads:
            if j is not None:
                kblk = main_block(km_ref, j, h)
            elif diag_in_main:
                kblk = main_block(km_ref, qi, h)
            else:
                kblk = kd_ref[0, :, _slab(h)].astype(BF16)
            for qm in qs[h]:
                zs.append(lax.dot_general(kblk, qm, NT_DIMS, preferred_element_type=F32))
        return zs

    def front_vpu(zs, j, sink, items=range(nmap * DIFF_HEADS)):
        bias_ref = bd_ref if j is None else bm_ref
        for idx in items:
            s = zs[idx] + bias_ref[idx // nmap]
            sink(idx, s, jnp.max(s, axis=0, keepdims=True))
        if j is None:
            return jnp.zeros((), F32), jnp.asarray(qi, jnp.int32)
        return jnp.asarray(q0 - j * tk, F32), jnp.asarray(j, jnp.int32)

    def back_mm(source, diag_ref):
        return None

    def back_vpu(source, pre, scalars, diag_ref, items=range(nmap * DIFF_HEADS)):
        dist0, j = scalars
        for idx in items:
            h = idx // nmap
            vblk = vd_ref[0, :, _slab(h)].astype(BF16) if diag_ref else main_block(vm_ref, j, h)
            off = slopes[h] * dist0
            m = m_ref[idx]
            m_new = jnp.maximum(m, source(idx, 1) - off)
            p = jnp.exp2(source(idx, 0) - (m_new + off))
            alpha = jnp.exp2(m - m_new)
            m_ref[idx] = m_new
            l_ref[idx] = alpha * l_ref[idx] + jnp.sum(p, axis=0, keepdims=True)
            pv = lax.dot_general(vblk, p.astype(BF16), TN_DIMS, preferred_element_type=F32)
            acc_ref[idx] = alpha * acc_ref[idx] + pv

    m_ref[...] = jnp.full(m_ref.shape, NEG, F32)
    l_ref[...] = jnp.zeros(l_ref.shape, F32)
    acc_ref[...] = jnp.zeros(acc_ref.shape, F32)
    _run_pipeline((front_mm, front_vpu, back_mm, back_vpu), (s_ref, mt_ref),
                  n_main=nm, diag_in_main=diag_in_main)

    for h in heads:
        if stacked:
            o = acc_ref[h] * (1.0 / l_ref[h])
            o = o[:, 0:tq] - lam * o[:, tq:nv]
        else:
            o = (acc_ref[2 * h] * (1.0 / l_ref[2 * h])
                 - acc_ref[2 * h + 1] * (lam / l_ref[2 * h + 1]))
        ms = jnp.mean(o * o, axis=0, keepdims=True)
        o = o * lax.rsqrt(ms + EPS) * g_ref[...] * (1.0 - LAMBDA_INIT)
        o_ref[0, :, _slab(h)] = o.T.astype(o_ref.dtype)


def _softplus2(z):
    return jnp.maximum(z, jnp.log(1.0 + jnp.exp2(jnp.minimum(z, 126.0))) * LOG2E)


def _sb_kernel(q_ref, km_ref, vm_ref, um_ref, o_ref, zt_ref, tb_ref, t0_ref, carry_ref, acc_ref,
               *, tk):
    tq = q_ref.shape[1]
    qi = pl.program_id(1)
    slabs = range(SB_HEADS // 2)
    qs = [_half_masks(q_ref[0, :, _slab(s)]) for s in slabs]

    def block(ref, j, s):
        return ref[0, pl.ds(pl.multiple_of(j * tk, tk), tk), _slab(s)]

    def front_mm(j):
        zs = []
        for s in slabs:
            kblk = block(km_ref, qi if j is None else j, s)
            for qm in qs[s]:
                zs.append(lax.dot_general(kblk, qm, NT_DIMS, preferred_element_type=F32))
        return zs

    def front_vpu(zs, j, sink, items=range(SB_HEADS)):
        earlier = None
        if j is None:
            earlier = (lax.broadcasted_iota(jnp.int32, (tk, tq), 0)
                       < lax.broadcasted_iota(jnp.int32, (tk, tq), 1))
        for idx in items:
            z = zs[idx]
            t = _softplus2(z)
            zt = z - t
            if earlier is not None:
                t = jnp.where(earlier, t, 0.0)
                zt = jnp.where(earlier, zt, NEG)
            tb = t.astype(BF16)
            sink(idx, zt, tb, tb[0:1, :].astype(F32))
        return (jnp.asarray(qi if j is None else j, jnp.int32),)

    def back_mm(source, diag_ref):
        return [jnp.dot(um_ref[...], source(idx, 1), preferred_element_type=F32)
                for idx in range(SB_HEADS)]

    def back_vpu(source, laters, scalars, diag_ref, items=range(SB_HEADS)):
        (j,) = scalars
        for idx in items:
            vblk = block(vm_ref, j, idx // 2)
            carry = carry_ref[idx]
            later = laters[idx]
            w = jnp.exp2(source(idx, 0) - later - carry)
            pv = lax.dot_general(vblk, w.astype(BF16), TN_DIMS, preferred_element_type=F32)
            carry_ref[idx] = carry + later[0:1, :] + source(idx, 2)
            acc_ref[idx] += pv

    carry_ref[...] = jnp.zeros(carry_ref.shape, F32)
    acc_ref[...] = jnp.zeros(acc_ref.shape, F32)
    _run_pipeline((front_mm, front_vpu, back_mm, back_vpu), (zt_ref, tb_ref, t0_ref),
                  n_main=qi, diag_in_main=True)
    row = lax.broadcasted_iota(jnp.int32, (SLAB, tq), 0)
    for s in slabs:
        o = jnp.where(row < HEAD_DIM, acc_ref[2 * s], acc_ref[2 * s + 1])
        o_ref[0, :, _slab(s)] = o.T.astype(o_ref.dtype)


def _sb_cache_kernel(q_ref, kt_ref, vt_ref, kn_ref, vn_ref, um_ref, ud_ref, o_ref, *, tk):
    tq = q_ref.shape[1]
    nchunk = kt_ref.shape[2] // tk
    slabs = range(SB_HEADS // 2)
    qv = []
    for s in slabs:
        qa, qb = _half_masks(q_ref[0, :, _slab(s)])
        qv.append(jnp.concatenate([qa, qb], axis=0))

    def chunk(ref, s, j):
        return ref[0, _slab(s), j * tk:(j + 1) * tk].astype(BF16)

    def front(j):
        earlier = None
        if j is None:
            r = lax.broadcasted_iota(jnp.int32, (2 * tq, tq), 0) & (tq - 1)
            earlier = lax.broadcasted_iota(jnp.int32, (2 * tq, tq), 1) < r
        zs = []
        for s in slabs:
            if j is None:
                zs.append(lax.dot_general(qv[s], kn_ref[0, :, _slab(s)], NT_DIMS,
                                          preferred_element_type=F32))
            else:
                zs.append(jnp.dot(qv[s], chunk(kt_ref, s, j), preferred_element_type=F32))
        staged = []
        for z in zs:
            t = _softplus2(z)
            zt = z - t
            if earlier is not None:
                t = jnp.where(earlier, t, 0.0)
                zt = jnp.where(earlier, zt, NEG)
            staged.append((zt, t.astype(BF16)))
        return staged

    def back(staged, j, state):
        u = ud_ref[...] if j is None else um_ref[...]
        tb_all = jnp.concatenate([tb for _, tb in staged], axis=0)
        later_all = jnp.dot(tb_all, u, preferred_element_type=F32)
        new_state = []
        for s in slabs:
            zt, tb = staged[s]
            carry, acc = state[s]
            later = later_all[s * 2 * tq:(s + 1) * 2 * tq]
            w = jnp.exp2(zt - later - carry).astype(BF16)
            if j is None:
                pv = jnp.dot(w, vn_ref[0, :, _slab(s)], preferred_element_type=F32)
            else:
                pv = lax.dot_general(w, chunk(vt_ref, s, j), NT_DIMS, preferred_element_type=F32)
            total = later[:, 0:1] + tb[:, 0:1].astype(F32)
            new_state.append((carry + total, acc + pv))
        return new_state

    state = [(jnp.zeros((2 * tq, 1), F32), jnp.zeros((2 * tq, SLAB), F32)) for _ in slabs]
    order = [None] + list(range(nchunk - 1, -1, -1))
    staged = front(order[0])
    for prev, nxt in zip(order[:-1], order[1:]):
        new = front(nxt)
        state = back(staged, prev, state)
        staged = new
    state = back(staged, order[-1], state)

    lane = lax.broadcasted_iota(jnp.int32, (tq, SLAB), 1)
    for s in slabs:
        acc = state[s][1]
        o_ref[0, :, _slab(s)] = jnp.where(lane < HEAD_DIM, acc[0:tq], acc[tq:2 * tq]).astype(o_ref.dtype)


def _later_keys(n, keys_on_rows):
    r = np.arange(n)
    u = (r[None, :] > r[:, None]) if keys_on_rows else (r[:, None] > r[None, :])
    return jnp.asarray(u.astype(np.float32), dtype=BF16)


def _alibi_tables(tq, tk, td, stacked):
    nv = 2 * tq if stacked else tq
    qc = np.arange(nv) % tq
    slopes = np.asarray(ALIBI_SLOPES, np.float64)[:, None, None] * LOG2E
    main = -slopes * (qc[None, :] - np.arange(tk)[:, None])[None]
    r = np.arange(td)[:, None]
    diag = np.where((r // CHUNK) <= (qc[None, :] // CHUNK), -slopes * np.abs(qc[None, :] - r)[None], NEG)
    return jnp.asarray(main, F32), jnp.asarray(diag, F32)


def _prompt_attention(q, kb, vb, lam, subln_g, *, tq):
    b, t, _ = q.shape
    grid = (b, t // tq)
    cparams = pltpu.CompilerParams(dimension_semantics=("parallel", "arbitrary"),
                                   vmem_limit_bytes=VMEM_LIMIT)
    smem = pl.BlockSpec(memory_space=pltpu.SMEM)
    g_spec = pl.BlockSpec((SLAB, 1), lambda i, j: (0, 0))
    out_spec = pl.BlockSpec((1, tq, DIFF_WIDTH), lambda i, j: (i, j, 0))
    out_shape = jax.ShapeDtypeStruct((b, t, DIFF_WIDTH), BF16)

    def specs(half):
        return (pl.BlockSpec((1, tq, DIFF_WIDTH), lambda i, j: (i, j, half)),
                pl.BlockSpec((1, t, DIFF_WIDTH), lambda i, j: (i, 0, half)))

    q_spec, kv_spec = specs(0)
    tab_spec = pl.BlockSpec((DIFF_HEADS, tq, tq), lambda i, j: (0, 0, 0))
    od = pl.pallas_call(
        functools.partial(_diff_kernel, tk=tq, td=tq, n_main=lambda qi: qi,
                          q_base=lambda qi: qi * tq, diag_base=lambda qi: qi * tq,
                          diag_in_main=True, stacked=False, interleaved=False),
        grid=grid, in_specs=[smem, q_spec, kv_spec, kv_spec, g_spec, tab_spec, tab_spec],
        out_specs=out_spec, out_shape=out_shape,
        scratch_shapes=[pltpu.VMEM((2, 2 * DIFF_HEADS, tq, tq), F32),
                        pltpu.VMEM((2, 2 * DIFF_HEADS, 1, tq), F32),
                        pltpu.VMEM((2 * DIFF_HEADS, 1, tq), F32),
                        pltpu.VMEM((2 * DIFF_HEADS, 1, tq), F32),
                        pltpu.VMEM((2 * DIFF_HEADS, SLAB, tq), F32)],
        compiler_params=cparams, name="diff_attention",
    )(lam, q, kb, vb, subln_g, *_alibi_tables(tq, tq, tq, False))

    q_spec, kv_spec = specs(1)
    osb = pl.pallas_call(
        functools.partial(_sb_kernel, tk=tq),
        grid=grid,
        in_specs=[q_spec, kv_spec, kv_spec, pl.BlockSpec((tq, tq), lambda i, j: (0, 0))],
        out_specs=out_spec, out_shape=out_shape,
        scratch_shapes=[pltpu.VMEM((2, SB_HEADS, tq, tq), F32),
                        pltpu.VMEM((2, SB_HEADS, tq, tq), BF16),
                        pltpu.VMEM((2, SB_HEADS, 1, tq), F32),
                        pltpu.VMEM((SB_HEADS, 1, tq), F32),
                        pltpu.VMEM((SB_HEADS, SLAB, tq), F32)],
        compiler_params=cparams, name="sb_attention",
    )(q, kb, vb, _later_keys(tq, True))
    return od, osb


def _sample_attention(q, kb, vb, cdk, cdv, cskt, csvt, lam, subln_g, *, tk):
    b, t, _ = q.shape
    past = cskt.shape[2]
    smem = pl.BlockSpec(memory_space=pltpu.SMEM)
    new = lambda half: pl.BlockSpec((1, t, DIFF_WIDTH), lambda i, j: (i, 0, half))
    out_spec = pl.BlockSpec((1, t, DIFF_WIDTH), lambda i, j: (i, 0, 0))
    out_shape = jax.ShapeDtypeStruct((b, t, DIFF_WIDTH), BF16)
    cparams = pltpu.CompilerParams(dimension_semantics=("parallel", "arbitrary"),
                                   vmem_limit_bytes=VMEM_LIMIT)

    rows = pl.BlockSpec((1, past * DIFF_HEADS, SLAB), lambda i, j: (i, 0, 0))
    od = pl.pallas_call(
        functools.partial(_diff_kernel, tk=tk, td=t, n_main=lambda qi: past // tk,
                          q_base=lambda qi: past, diag_base=lambda qi: past,
                          diag_in_main=False, stacked=True, interleaved=True),
        grid=(b, 1),
        in_specs=[smem, new(0), rows, rows, new(0), new(0),
                  pl.BlockSpec((SLAB, 1), lambda i, j: (0, 0)),
                  pl.BlockSpec((DIFF_HEADS, tk, 2 * t), lambda i, j: (0, 0, 0)),
                  pl.BlockSpec((DIFF_HEADS, t, 2 * t), lambda i, j: (0, 0, 0))],
        out_specs=out_spec, out_shape=out_shape,
        scratch_shapes=[pltpu.VMEM((2, DIFF_HEADS, tk, 2 * t), F32),
                        pltpu.VMEM((2, DIFF_HEADS, 1, 2 * t), F32),
                        pltpu.VMEM((DIFF_HEADS, 1, 2 * t), F32),
                        pltpu.VMEM((DIFF_HEADS, 1, 2 * t), F32),
                        pltpu.VMEM((DIFF_HEADS, SLAB, 2 * t), F32)],
        compiler_params=cparams, name="diff_attention_cache",
    )(lam, q, cdk, cdv, kb, vb, subln_g, *_alibi_tables(t, tk, t, True))

    feat = pl.BlockSpec((1, SB_WIDTH, past), lambda i, j: (i, 0, 0))
    osb = pl.pallas_call(
        functools.partial(_sb_cache_kernel, tk=tk),
        grid=(b, 1),
        in_specs=[new(1), feat, feat, new(1), new(1),
                  pl.BlockSpec((tk, tk), lambda i, j: (0, 0)),
                  pl.BlockSpec((t, t), lambda i, j: (0, 0))],
        out_specs=out_spec, out_shape=out_shape,
        compiler_params=cparams, name="sb_attention_cache",
    )(q, cskt, csvt, kb, vb, _later_keys(tk, False), _later_keys(t, False))
    return od, osb


def _out_kernel(od_ref, osb_ref, x_ref, mod_ref, gpm_ref, gpf_ref, gqf_ref,
                wo_ref, wu_ref, wd_ref, y_ref):
    bb, tm, d = x_ref.shape
    n = bb * tm
    od = od_ref[...].reshape(n, DIFF_WIDTH)
    osb = osb_ref[...].reshape(n, SB_WIDTH)
    y = (jnp.dot(od, wo_ref[0:DIFF_WIDTH, :], preferred_element_type=F32)
         + jnp.dot(osb, wo_ref[DIFF_WIDTH:MIX_WIDTH, :], preferred_element_type=F32))
    y = y.reshape(bb, tm, d)
    gate1 = mod_ref[:, 2:3, :]
    shift2 = mod_ref[:, 3:4, :]
    scale2 = mod_ref[:, 4:5, :]
    gate2 = mod_ref[:, 5:6, :]
    x1 = x_ref[...] + gate1 * _rms(y, gpm_ref[...])
    h2 = _rms(x1, gpf_ref[...]) * (1.0 + scale2) + shift2
    h2b = h2.reshape(n, d).astype(BF16)
    acc = jnp.zeros((n, d), F32)
    fc = 1024
    for c in range(D_FF // fc):
        f = jnp.dot(h2b, wu_ref[:, c * fc:(c + 1) * fc], preferred_element_type=F32)
        r = jnp.square(jnp.maximum(f, 0.0)).astype(BF16)
        acc = acc + jnp.dot(r, wd_ref[c * fc:(c + 1) * fc, :], preferred_element_type=F32)
    y2 = acc.reshape(bb, tm, d)
    y_ref[...] = x1 + gate2 * _rms(y2, gqf_ref[...])


def _out_ffn(od, osb, x, mods, mod_off, g_post_mix, g_pre_ffn, g_post_ffn,
             wo_bf, wu_bf, wd_bf, bb, tm):
    b, s, d = x.shape
    grid = (b // bb, s // tm)
    tok = lambda w: pl.BlockSpec((bb, tm, w), lambda i, t: (i, t, 0))
    const = lambda shape: pl.BlockSpec(shape, lambda i, t: (0,) * len(shape),
                                       pipeline_mode=pl.Buffered(1))
    return pl.pallas_call(
        _out_kernel,
        grid=grid,
        in_specs=[tok(DIFF_WIDTH), tok(SB_WIDTH), tok(d),
                  pl.BlockSpec((bb, 6, d), lambda i, t: (i + mod_off // bb, 0, 0)),
                  const((1, d)), const((1, d)), const((1, d)),
                  const((MIX_WIDTH, d)), const((d, D_FF)), const((D_FF, d))],
        out_specs=tok(d),
        out_shape=jax.ShapeDtypeStruct((b, s, d), F32),
        compiler_params=pltpu.CompilerParams(dimension_semantics=("parallel", "parallel"),
                                             vmem_limit_bytes=VMEM_LIMIT),
        name="out_ffn",
    )(od, osb, x, mods, g_post_mix, g_pre_ffn, g_post_ffn, wo_bf, wu_bf, wd_bf)


def kernel(x_prompt, x_sample, c_prompt, c_sample, cache_diff_k, cache_diff_v, cache_sb_k, cache_sb_v,
           w_ada, b_ada, g_pre_mix, g_post_mix, w_in, lambda_q1, lambda_k1, lambda_q2, lambda_k2,
           diff_subln_g, w_out, g_pre_ffn, g_post_ffn, w_up, w_down):
    bp, sp, d = x_prompt.shape
    bs, ss, _ = x_sample.shape
    past = cache_diff_k.shape[2]
    l = 0

    c_all = jnp.concatenate([c_prompt, c_sample], axis=0)
    mods, lam_tile = _modulation(c_all, w_ada[l], b_ada[l][None, :],
                                 lambda_q1[l][None, :], lambda_k1[l][None, :],
                                 lambda_q2[l][None, :], lambda_k2[l][None, :])
    mods = mods.reshape(bp + bs, 6, d)
    lam = lam_tile[0, 0:1]
    subln_g = diff_subln_g[l].reshape(SLAB, 1)

    w_in_bf = w_in[l].astype(BF16)
    w_out_bf = w_out[l].astype(BF16)
    w_up_bf = w_up[l].astype(BF16)
    w_down_bf = w_down[l].astype(BF16)
    g1 = g_pre_mix[l][None, :]
    g2 = g_post_mix[l][None, :]
    g3 = g_pre_ffn[l][None, :]
    g4 = g_post_ffn[l][None, :]
    diff_shape = lambda b, s: (1, b, s, DIFF_HEADS, 2 * HEAD_DIM)
    sb_shape = lambda b, s: (1, b, s, SB_HEADS, HEAD_DIM)
    sb_result = lambda a, b, s: jnp.transpose(a.reshape(b, SB_HEADS, HEAD_DIM, s),
                                              (0, 3, 1, 2)).reshape(sb_shape(b, s))

    q, kb, vb, kd, vd, ks, vs = _in_proj(x_prompt, mods, 0, g1, w_in_bf, bb=1, tm=512,
                                         sb_feature_major=True)
    od, osb = _prompt_attention(q, kb, vb, lam, subln_g, tq=256)
    y_prompt = _out_ffn(od, osb, x_prompt, mods, 0, g2, g3, g4,
                        w_out_bf, w_up_bf, w_down_bf, bb=1, tm=512)
    prompt_kv = (kd.reshape(diff_shape(bp, sp)), vd.reshape(diff_shape(bp, sp)),
                 sb_result(ks, bp, sp), sb_result(vs, bp, sp))

    q2, kb2, vb2, kd2, vd2, ks2, vs2 = _in_proj(x_sample, mods, bp, g1, w_in_bf, bb=8, tm=ss,
                                                sb_feature_major=False)
    cdk = cache_diff_k[l].reshape(bs, past * DIFF_HEADS, SLAB)
    cdv = cache_diff_v[l].reshape(bs, past * DIFF_HEADS, SLAB)
    cskt = jnp.transpose(cache_sb_k[l], (0, 2, 3, 1)).reshape(bs, SB_WIDTH, past)
    csvt = jnp.transpose(cache_sb_v[l], (0, 2, 3, 1)).reshape(bs, SB_WIDTH, past)
    od2, osb2 = _sample_attention(q2, kb2, vb2, cdk, cdv, cskt, csvt, lam, subln_g, tk=256)
    y_sample = _out_ffn(od2, osb2, x_sample, mods, bp, g2, g3, g4,
                        w_out_bf, w_up_bf, w_down_bf, bb=8, tm=ss)
    sample_kv = (kd2.reshape(diff_shape(bs, ss)), vd2.reshape(diff_shape(bs, ss)),
                 ks2.reshape(sb_shape(bs, ss)), vs2.reshape(sb_shape(bs, ss)))

    return (y_prompt, y_sample, *prompt_kv, *sample_kv)
```

```python
import functools
import math

import jax
import jax.numpy as jnp
import numpy as np
from jax import lax
from jax.experimental import pallas as pl
from jax.experimental.pallas import tpu as pltpu

D_MODEL = 1024
CHUNK = 64
HEAD_DIM = 64
DIFF_HEADS = 4
SB_HEADS = 8
SLAB = 128
DIFF_WIDTH = DIFF_HEADS * 2 * HEAD_DIM
SB_WIDTH = SB_HEADS * HEAD_DIM
MIX_WIDTH = DIFF_WIDTH + SB_WIDTH
IN_WIDTH = 3 * MIX_WIDTH
D_FF = 4 * D_MODEL
EPS = 1e-6
NEG = -1e30
LAMBDA_INIT = 0.8 - 0.6 * math.exp(-0.3 * 0)
ATTN_SCALE = HEAD_DIM ** -0.5
LOG2E = math.log2(math.e)
Q_SCALE = ATTN_SCALE * LOG2E
DEAD_LOG2 = 160.0
ALIBI_SLOPES = tuple(float(2.0 ** (-8.0 * (i + 1) / DIFF_HEADS)) for i in range(DIFF_HEADS))

VMEM_LIMIT = 56 * 1024 * 1024
BF16 = jnp.bfloat16
F32 = jnp.float32

NT_DIMS = (((1,), (1,)), ((), ()))
TN_DIMS = (((0,), (0,)), ((), ()))


def _rms(x, g):
    ms = jnp.mean(x * x, axis=-1, keepdims=True)
    return x * lax.rsqrt(ms + EPS) * g


def _slab(i):
    return slice(i * SLAB, (i + 1) * SLAB)


def _mod_kernel(c_ref, w_ref, b_ref, lq1_ref, lk1_ref, lq2_ref, lk2_ref, m_ref, lam_ref):
    c = c_ref[...]
    s = c * jax.nn.sigmoid(c)
    m_ref[...] = jnp.dot(s, w_ref[...], preferred_element_type=F32,
                         precision=lax.Precision.HIGHEST) + b_ref[...]
    d1 = jnp.sum(lq1_ref[...] * lk1_ref[...], axis=-1, keepdims=True)
    d2 = jnp.sum(lq2_ref[...] * lk2_ref[...], axis=-1, keepdims=True)
    lam = jnp.exp(d1) - jnp.exp(d2) + LAMBDA_INIT
    lam_ref[...] = jnp.broadcast_to(lam, lam_ref.shape)


def _modulation(c_all, w_ada, b_ada, lq1, lk1, lq2, lk2):
    nb = c_all.shape[0]
    tn = 1024
    vec = pl.BlockSpec((1, HEAD_DIM), lambda j: (0, 0))
    return pl.pallas_call(
        _mod_kernel,
        grid=(6 * D_MODEL // tn,),
        in_specs=[pl.BlockSpec((nb, D_MODEL), lambda j: (0, 0)),
                  pl.BlockSpec((D_MODEL, tn), lambda j: (0, j)),
                  pl.BlockSpec((1, tn), lambda j: (0, j)),
                  vec, vec, vec, vec],
        out_specs=[pl.BlockSpec((nb, tn), lambda j: (0, j)),
                   pl.BlockSpec((8, 128), lambda j: (0, 0))],
        out_shape=[jax.ShapeDtypeStruct((nb, 6 * D_MODEL), F32),
                   jax.ShapeDtypeStruct((8, 128), F32)],
        compiler_params=pltpu.CompilerParams(dimension_semantics=("arbitrary",),
                                             vmem_limit_bytes=VMEM_LIMIT),
        name="modulation",
    )(c_all, w_ada, b_ada, lq1, lk1, lq2, lk2)


def _in_kernel(x_ref, mod_ref, g_ref, w_ref,
               q_ref, kb_ref, vb_ref, kd_ref, vd_ref, ks_ref, vs_ref, *, sb_feature_major):
    bb, tm, d = x_ref.shape
    x = x_ref[...]
    shift = mod_ref[:, 0:1, :]
    scale = mod_ref[:, 1:2, :]
    h = _rms(x, g_ref[...]) * (1.0 + scale) + shift
    hb = h.reshape(bb * tm, d).astype(BF16)

    def proj(c):
        u = jnp.dot(hb, w_ref[:, c * 512:(c + 1) * 512], preferred_element_type=F32)
        return u.reshape(bb, tm, 512)

    q_ref[:, :, 0:512] = (proj(0) * Q_SCALE).astype(BF16)
    q_ref[:, :, 512:1024] = (proj(3) * Q_SCALE).astype(BF16)
    for c, f_ref, b_ref in ((1, kd_ref, kb_ref), (2, vd_ref, vb_ref)):
        u = proj(c)
        b_ref[:, :, 0:512] = u.astype(BF16)
        for hd in range(DIFF_HEADS):
            f_ref[:, pl.ds(hd, tm, stride=DIFF_HEADS), :] = u[:, :, _slab(hd)]
    for c, f_ref, b_ref in ((4, ks_ref, kb_ref), (5, vs_ref, vb_ref)):
        u = proj(c)
        b_ref[:, :, 512:1024] = u.astype(BF16)
        if sb_feature_major:
            for i in range(bb):
                f_ref[i] = u[i].T
        else:
            f_ref[...] = u


def _in_proj(x, mods, mod_off, g, w_bf, bb, tm, sb_feature_major):
    b, s, d = x.shape
    grid = (b // bb, s // tm)
    tok = lambda w: pl.BlockSpec((bb, tm, w), lambda i, t: (i, t, 0))
    rows = pl.BlockSpec((bb, tm * DIFF_HEADS, SLAB), lambda i, t: (i, t, 0))
    const = lambda shape: pl.BlockSpec(shape, lambda i, t: (0,) * len(shape),
                                       pipeline_mode=pl.Buffered(1))
    outs = [jax.ShapeDtypeStruct((b, s, MIX_WIDTH), BF16)] * 3 + \
           [jax.ShapeDtypeStruct((b, s * DIFF_HEADS, SLAB), F32)] * 2 + \
           [jax.ShapeDtypeStruct((b, SB_WIDTH, s) if sb_feature_major else (b, s, SB_WIDTH), F32)] * 2
    sb_spec = (pl.BlockSpec((bb, SB_WIDTH, tm), lambda i, t: (i, 0, t)) if sb_feature_major
               else tok(SB_WIDTH))
    return pl.pallas_call(
        functools.partial(_in_kernel, sb_feature_major=sb_feature_major),
        grid=grid,
        in_specs=[tok(d),
                  pl.BlockSpec((bb, 6, d), lambda i, t: (i + mod_off // bb, 0, 0)),
                  const((1, d)),
                  const((d, IN_WIDTH))],
        out_specs=[tok(MIX_WIDTH)] * 3 + [rows] * 2 + [sb_spec] * 2,
        out_shape=outs,
        compiler_params=pltpu.CompilerParams(dimension_semantics=("parallel", "parallel"),
                                             vmem_limit_bytes=VMEM_LIMIT),
        name="in_proj",
    )(x, mods, g, w_bf)


def _half_masks(q):
    lane = lax.broadcasted_iota(jnp.int32, q.shape, 1)
    zero = jnp.zeros_like(q)
    return jnp.where(lane < HEAD_DIM, q, zero), jnp.where(lane >= HEAD_DIM, q, zero)


def _staging(refs, slot):
    def sink(idx, *vals):
        for r, v in zip(refs, vals):
            r[slot, idx] = v

    def source(idx, k):
        return refs[k][slot, idx]
    return sink, source


def _list_staging():
    store = {}

    def sink(idx, *vals):
        store[idx] = vals

    def source(idx, k):
        return store[idx][k]
    return sink, source


def _run_pipeline(stages, stage_refs, *, n_main, diag_in_main, finished=None):
    front_mm, front_vpu, back_mm, back_vpu = stages

    def body(i, scalars):
        sink, _ = _staging(stage_refs, (i + 1) % 2)
        _, src = _staging(stage_refs, i % 2)
        j_next = n_main - 1 - i
        pre = back_mm(src, False)
        zs = front_mm(j_next)
        back_vpu(src, pre, scalars, False)
        return front_vpu(zs, j_next, sink)

    if diag_in_main:
        sink, _ = _staging(stage_refs, 0)
        scalars = front_vpu(front_mm(None), None, sink)
        first = 0
    else:
        sink, src = _list_staging()
        scalars = front_vpu(front_mm(None), None, sink)
        back_vpu(src, back_mm(src, True), scalars, True)
        sink, _ = _staging(stage_refs, 1)
        scalars = front_vpu(front_mm(n_main - 1), n_main - 1, sink)
        first = 1

    if finished is None:
        scalars = lax.fori_loop(first, n_main, body, scalars)
        _, src = _staging(stage_refs, n_main % 2)
        back_vpu(src, back_mm(src, False), scalars, False)
        return

    def w_cond(carry):
        i, done, _ = carry
        return jnp.logical_and(i < n_main, jnp.logical_not(done))

    def w_body(carry):
        i, _, scalars = carry
        scalars = body(i, scalars)
        return i + 1, finished(), scalars

    i, done, scalars = lax.while_loop(w_cond, w_body, (jnp.int32(first), finished(), scalars))

    @pl.when(jnp.logical_not(done))
    def _():
        _, src = _staging(stage_refs, i % 2)
        back_vpu(src, back_mm(src, False), scalars, False)


def _diff_kernel(*refs, tk, td, n_main, q_base, diag_base, diag_in_main, stacked, interleaved):
    if diag_in_main:
        (lam_ref, q_ref, km_ref, vm_ref, g_ref, bm_ref, bd_ref, o_ref,
         s_ref, mt_ref, m_ref, l_ref, acc_ref) = refs
        kd_ref = vd_ref = None
    else:
        (lam_ref, q_ref, km_ref, vm_ref, kd_ref, vd_ref, g_ref, bm_ref, bd_ref, o_ref,
         s_ref, mt_ref, m_ref, l_ref, acc_ref) = refs
    tq = q_ref.shape[1]
    nmap = 1 if stacked else 2
    nv = 2 * tq if stacked else tq
    qi = pl.program_id(1)
    lam = lam_ref[0]
    q0 = q_base(qi)
    nm = n_main(qi)
    heads = range(DIFF_HEADS)
    slopes = [s * LOG2E for s in ALIBI_SLOPES]
    qs = []
    for h in heads:
        q1, q2 = _half_masks(q_ref[0, :, _slab(h)])
        qs.append([jnp.concatenate([q1, q2], axis=0)] if stacked else [q1, q2])

    def main_block(ref, j, h):
        if interleaved:
            start = pl.multiple_of(j * (tk * DIFF_HEADS), tk * DIFF_HEADS) + h
            return ref[0, pl.ds(start, tk, stride=DIFF_HEADS), :].astype(BF16)
        return ref[0, pl.ds(pl.multiple_of(j * tk, tk), tk), _slab(h)].astype(BF16)

    def front_mm(j):
        zs = []
        for h in heads:
            if j is not None:
                kblk = main_block(km_ref, j, h)
            elif diag_in_main:
                kblk = main_block(km_ref, qi, h)
            else:
                kblk = kd_ref[0, :, _slab(h)].astype(BF16)
            for qm in qs[h]:
                zs.append(lax.dot_general(kblk, qm, NT_DIMS, preferred_element_type=F32))
        return zs

    def front_vpu(zs, j, sink, items=range(nmap * DIFF_HEADS)):
        bias_ref = bd_ref if j is None else bm_ref
        for idx in items:
            s = zs[idx] + bias_ref[idx // nmap]
            sink(idx, s, jnp.max(s, axis=0, keepdims=True))
        if j is None:
            return jnp.zeros((), F32), jnp.asarray(qi, jnp.int32)
        return jnp.asarray(q0 - j * tk, F32), jnp.asarray(j, jnp.int32)

    def back_mm(source, diag_ref):
        return None

    def back_vpu(source, pre, scalars, diag_ref, items=range(nmap * DIFF_HEADS)):
        dist0, j = scalars
        for idx in items:
            h = idx // nmap
            vblk = vd_ref[0, :, _slab(h)].astype(BF16) if diag_ref else main_block(vm_ref, j, h)
            off = slopes[h] * dist0
            m = m_ref[idx]
            m_new = jnp.maximum(m, source(idx, 1) - off)
            p = jnp.exp2(source(idx, 0) - (m_new + off))
            alpha = jnp.exp2(m - m_new)
            m_ref[idx] = m_new
            l_ref[idx] = alpha * l_ref[idx] + jnp.sum(p, axis=0, keepdims=True)
            pv = lax.dot_general(vblk, p.astype(BF16), TN_DIMS, preferred_element_type=F32)
            acc_ref[idx] = alpha * acc_ref[idx] + pv

    m_ref[...] = jnp.full(m_ref.shape, NEG, F32)
    l_ref[...] = jnp.zeros(l_ref.shape, F32)
    acc_ref[...] = jnp.zeros(acc_ref.shape, F32)
    _run_pipeline((front_mm, front_vpu, back_mm, back_vpu), (s_ref, mt_ref),
                  n_main=nm, diag_in_main=diag_in_main)

    for h in heads:
        if stacked:
            o = acc_ref[h] * (1.0 / l_ref[h])
            o = o[:, 0:tq] - lam * o[:, tq:nv]
        else:
            o = (acc_ref[2 * h] * (1.0 / l_ref[2 * h])
                 - acc_ref[2 * h + 1] * (lam / l_ref[2 * h + 1]))
        ms = jnp.mean(o * o, axis=0, keepdims=True)
        o = o * lax.rsqrt(ms + EPS) * g_ref[...] * (1.0 - LAMBDA_INIT)
        o_ref[0, :, _slab(h)] = o.T.astype(o_ref.dtype)


def _softplus2(z):
    return jnp.maximum(z, jnp.log(1.0 + jnp.exp2(jnp.minimum(z, 126.0))) * LOG2E)


def _sb_kernel(q_ref, km_ref, vm_ref, um_ref, o_ref, zt_ref, tb_ref, t0_ref, carry_ref, acc_ref,
               *, tk):
    tq = q_ref.shape[1]
    qi = pl.program_id(1)
    slabs = range(SB_HEADS // 2)
    qs = [_half_masks(q_ref[0, :, _slab(s)]) for s in slabs]

    def block(ref, j, s):
        return ref[0, pl.ds(pl.multiple_of(j * tk, tk), tk), _slab(s)]

    def front_mm(j):
        zs = []
        for s in slabs:
            kblk = block(km_ref, qi if j is None else j, s)
            for qm in qs[s]:
                zs.append(lax.dot_general(kblk, qm, NT_DIMS, preferred_element_type=F32))
        return zs

    def front_vpu(zs, j, sink, items=range(SB_HEADS)):
        earlier = None
        if j is None:
            earlier = (lax.broadcasted_iota(jnp.int32, (tk, tq), 0)
                       < lax.broadcasted_iota(jnp.int32, (tk, tq), 1))
        for idx in items:
            z = zs[idx]
            t = _softplus2(z)
            zt = z - t
            if earlier is not None:
                t = jnp.where(earlier, t, 0.0)
                zt = jnp.where(earlier, zt, NEG)
            tb = t.astype(BF16)
            sink(idx, zt, tb, tb[0:1, :].astype(F32))
        return (jnp.asarray(qi if j is None else j, jnp.int32),)

    def back_mm(source, diag_ref):
        return [jnp.dot(um_ref[...], source(idx, 1), preferred_element_type=F32)
                for idx in range(SB_HEADS)]

    def back_vpu(source, laters, scalars, diag_ref, items=range(SB_HEADS)):
        (j,) = scalars
        for idx in items:
            vblk = block(vm_ref, j, idx // 2)
            carry = carry_ref[idx]
            later = laters[idx]
            w = jnp.exp2(source(idx, 0) - later - carry)
            pv = lax.dot_general(vblk, w.astype(BF16), TN_DIMS, preferred_element_type=F32)
            carry_ref[idx] = carry + later[0:1, :] + source(idx, 2)
            acc_ref[idx] += pv

    carry_ref[...] = jnp.zeros(carry_ref.shape, F32)
    acc_ref[...] = jnp.zeros(acc_ref.shape, F32)
    def finished():
        return jnp.min(carry_ref[...]) >= DEAD_LOG2

    _run_pipeline((front_mm, front_vpu, back_mm, back_vpu), (zt_ref, tb_ref, t0_ref),
                  n_main=qi, diag_in_main=True, finished=finished)
    row = lax.broadcasted_iota(jnp.int32, (SLAB, tq), 0)
    for s in slabs:
        o = jnp.where(row < HEAD_DIM, acc_ref[2 * s], acc_ref[2 * s + 1])
        o_ref[0, :, _slab(s)] = o.T.astype(o_ref.dtype)


def _sb_cache_kernel(q_ref, kt_ref, vt_ref, kn_ref, vn_ref, um_ref, ud_ref, o_ref, *, tk):
    tq = q_ref.shape[1]
    nchunk = kt_ref.shape[2] // tk
    slabs = range(SB_HEADS // 2)
    qv = []
    for s in slabs:
        qa, qb = _half_masks(q_ref[0, :, _slab(s)])
        qv.append(jnp.concatenate([qa, qb], axis=0))

    def chunk(ref, s, j):
        return ref[0, _slab(s), j * tk:(j + 1) * tk].astype(BF16)

    def front(j):
        earlier = None
        if j is None:
            r = lax.broadcasted_iota(jnp.int32, (2 * tq, tq), 0) & (tq - 1)
            earlier = lax.broadcasted_iota(jnp.int32, (2 * tq, tq), 1) < r
        zs = []
        for s in slabs:
            if j is None:
                zs.append(lax.dot_general(qv[s], kn_ref[0, :, _slab(s)], NT_DIMS,
                                          preferred_element_type=F32))
            else:
                zs.append(jnp.dot(qv[s], chunk(kt_ref, s, j), preferred_element_type=F32))
        staged = []
        for z in zs:
            t = _softplus2(z)
            zt = z - t
            if earlier is not None:
                t = jnp.where(earlier, t, 0.0)
                zt = jnp.where(earlier, zt, NEG)
            staged.append((zt, t.astype(BF16)))
        return staged

    def back(staged, j, state):
        u = ud_ref[...] if j is None else um_ref[...]
        tb_all = jnp.concatenate([tb for _, tb in staged], axis=0)
        later_all = jnp.dot(tb_all, u, preferred_element_type=F32)
        new_state = []
        for s in slabs:
            zt, tb = staged[s]
            carry, acc = state[s]
            later = later_all[s * 2 * tq:(s + 1) * 2 * tq]
            w = jnp.exp2(zt - later - carry).astype(BF16)
            if j is None:
                pv = jnp.dot(w, vn_ref[0, :, _slab(s)], preferred_element_type=F32)
            else:
                pv = lax.dot_general(w, chunk(vt_ref, s, j), NT_DIMS, preferred_element_type=F32)
            total = later[:, 0:1] + tb[:, 0:1].astype(F32)
            new_state.append((carry + total, acc + pv))
        return new_state

    state = [(jnp.zeros((2 * tq, 1), F32), jnp.zeros((2 * tq, SLAB), F32)) for _ in slabs]
    order = [None] + list(range(nchunk - 1, -1, -1))
    staged = front(order[0])
    for prev, nxt in zip(order[:-1], order[1:]):
        new = front(nxt)
        state = back(staged, prev, state)
        staged = new
    state = back(staged, order[-1], state)

    lane = lax.broadcasted_iota(jnp.int32, (tq, SLAB), 1)
    for s in slabs:
        acc = state[s][1]
        o_ref[0, :, _slab(s)] = jnp.where(lane < HEAD_DIM, acc[0:tq], acc[tq:2 * tq]).astype(o_ref.dtype)


def _later_keys(n, keys_on_rows):
    r = np.arange(n)
    u = (r[None, :] > r[:, None]) if keys_on_rows else (r[:, None] > r[None, :])
    return jnp.asarray(u.astype(np.float32), dtype=BF16)


def _alibi_tables(tq, tk, td, stacked):
    nv = 2 * tq if stacked else tq
    qc = np.arange(nv) % tq
    slopes = np.asarray(ALIBI_SLOPES, np.float64)[:, None, None] * LOG2E
    main = -slopes * (qc[None, :] - np.arange(tk)[:, None])[None]
    r = np.arange(td)[:, None]
    diag = np.where((r // CHUNK) <= (qc[None, :] // CHUNK), -slopes * np.abs(qc[None, :] - r)[None], NEG)
    return jnp.asarray(main, F32), jnp.asarray(diag, F32)


def _prompt_attention(q, kb, vb, lam, subln_g, *, tq):
    b, t, _ = q.shape
    grid = (b, t // tq)
    cparams = pltpu.CompilerParams(dimension_semantics=("parallel", "arbitrary"),
                                   vmem_limit_bytes=VMEM_LIMIT)
    smem = pl.BlockSpec(memory_space=pltpu.SMEM)
    g_spec = pl.BlockSpec((SLAB, 1), lambda i, j: (0, 0))
    out_spec = pl.BlockSpec((1, tq, DIFF_WIDTH), lambda i, j: (i, j, 0))
    out_shape = jax.ShapeDtypeStruct((b, t, DIFF_WIDTH), BF16)

    def specs(half):
        return (pl.BlockSpec((1, tq, DIFF_WIDTH), lambda i, j: (i, j, half)),
                pl.BlockSpec((1, t, DIFF_WIDTH), lambda i, j: (i, 0, half)))

    q_spec, kv_spec = specs(0)
    tab_spec = pl.BlockSpec((DIFF_HEADS, tq, tq), lambda i, j: (0, 0, 0))
    od = pl.pallas_call(
        functools.partial(_diff_kernel, tk=tq, td=tq, n_main=lambda qi: qi,
                          q_base=lambda qi: qi * tq, diag_base=lambda qi: qi * tq,
                          diag_in_main=True, stacked=False, interleaved=False),
        grid=grid, in_specs=[smem, q_spec, kv_spec, kv_spec, g_spec, tab_spec, tab_spec],
        out_specs=out_spec, out_shape=out_shape,
        scratch_shapes=[pltpu.VMEM((2, 2 * DIFF_HEADS, tq, tq), F32),
                        pltpu.VMEM((2, 2 * DIFF_HEADS, 1, tq), F32),
                        pltpu.VMEM((2 * DIFF_HEADS, 1, tq), F32),
                        pltpu.VMEM((2 * DIFF_HEADS, 1, tq), F32),
                        pltpu.VMEM((2 * DIFF_HEADS, SLAB, tq), F32)],
        compiler_params=cparams, name="diff_attention",
    )(lam, q, kb, vb, subln_g, *_alibi_tables(tq, tq, tq, False))

    q_spec, kv_spec = specs(1)
    osb = pl.pallas_call(
        functools.partial(_sb_kernel, tk=tq),
        grid=grid,
        in_specs=[q_spec, kv_spec, kv_spec, pl.BlockSpec((tq, tq), lambda i, j: (0, 0))],
        out_specs=out_spec, out_shape=out_shape,
        scratch_shapes=[pltpu.VMEM((2, SB_HEADS, tq, tq), F32),
                        pltpu.VMEM((2, SB_HEADS, tq, tq), BF16),
                        pltpu.VMEM((2, SB_HEADS, 1, tq), F32),
                        pltpu.VMEM((SB_HEADS, 1, tq), F32),
                        pltpu.VMEM((SB_HEADS, SLAB, tq), F32)],
        compiler_params=cparams, name="sb_attention",
    )(q, kb, vb, _later_keys(tq, True))
    return od, osb


def _sample_attention(q, kb, vb, cdk, cdv, cskt, csvt, lam, subln_g, *, tk):
    b, t, _ = q.shape
    past = cskt.shape[2]
    smem = pl.BlockSpec(memory_space=pltpu.SMEM)
    new = lambda half: pl.BlockSpec((1, t, DIFF_WIDTH), lambda i, j: (i, 0, half))
    out_spec = pl.BlockSpec((1, t, DIFF_WIDTH), lambda i, j: (i, 0, 0))
    out_shape = jax.ShapeDtypeStruct((b, t, DIFF_WIDTH), BF16)
    cparams = pltpu.CompilerParams(dimension_semantics=("parallel", "arbitrary"),
                                   vmem_limit_bytes=VMEM_LIMIT)

    rows = pl.BlockSpec((1, past * DIFF_HEADS, SLAB), lambda i, j: (i, 0, 0))
    od = pl.pallas_call(
        functools.partial(_diff_kernel, tk=tk, td=t, n_main=lambda qi: past // tk,
                          q_base=lambda qi: past, diag_base=lambda qi: past,
                          diag_in_main=False, stacked=True, interleaved=True),
        grid=(b, 1),
        in_specs=[smem, new(0), rows, rows, new(0), new(0),
                  pl.BlockSpec((SLAB, 1), lambda i, j: (0, 0)),
                  pl.BlockSpec((DIFF_HEADS, tk, 2 * t), lambda i, j: (0, 0, 0)),
                  pl.BlockSpec((DIFF_HEADS, t, 2 * t), lambda i, j: (0, 0, 0))],
        out_specs=out_spec, out_shape=out_shape,
        scratch_shapes=[pltpu.VMEM((2, DIFF_HEADS, tk, 2 * t), F32),
                        pltpu.VMEM((2, DIFF_HEADS, 1, 2 * t), F32),
                        pltpu.VMEM((DIFF_HEADS, 1, 2 * t), F32),
                        pltpu.VMEM((DIFF_HEADS, 1, 2 * t), F32),
                        pltpu.VMEM((DIFF_HEADS, SLAB, 2 * t), F32)],
        compiler_params=cparams, name="diff_attention_cache",
    )(lam, q, cdk, cdv, kb, vb, subln_g, *_alibi_tables(t, tk, t, True))

    feat = pl.BlockSpec((1, SB_WIDTH, past), lambda i, j: (i, 0, 0))
    osb = pl.pallas_call(
        functools.partial(_sb_cache_kernel, tk=tk),
        grid=(b, 1),
        in_specs=[new(1), feat, feat, new(1), new(1),
                  pl.BlockSpec((tk, tk), lambda i, j: (0, 0)),
                  pl.BlockSpec((t, t), lambda i, j: (0, 0))],
        out_specs=out_spec, out_shape=out_shape,
        compiler_params=cparams, name="sb_attention_cache",
    )(q, cskt, csvt, kb, vb, _later_keys(tk, False), _later_keys(t, False))
    return od, osb


def _out_kernel(od_ref, osb_ref, x_ref, mod_ref, gpm_ref, gpf_ref, gqf_ref,
                wo_ref, wu_ref, wd_ref, y_ref):
    bb, tm, d = x_ref.shape
    n = bb * tm
    od = od_ref[...].reshape(n, DIFF_WIDTH)
    osb = osb_ref[...].reshape(n, SB_WIDTH)
    y = (jnp.dot(od, wo_ref[0:DIFF_WIDTH, :], preferred_element_type=F32)
         + jnp.dot(osb, wo_ref[DIFF_WIDTH:MIX_WIDTH, :], preferred_element_type=F32))
    y = y.reshape(bb, tm, d)
    gate1 = mod_ref[:, 2:3, :]
    shift2 = mod_ref[:, 3:4, :]
    scale2 = mod_ref[:, 4:5, :]
    gate2 = mod_ref[:, 5:6, :]
    x1 = x_ref[...] + gate1 * _rms(y, gpm_ref[...])
    h2 = _rms(x1, gpf_ref[...]) * (1.0 + scale2) + shift2
    h2b = h2.reshape(n, d).astype(BF16)
    acc = jnp.zeros((n, d), F32)
    fc = 1024
    for c in range(D_FF // fc):
        f = jnp.dot(h2b, wu_ref[:, c * fc:(c + 1) * fc], preferred_element_type=F32)
        r = jnp.square(jnp.maximum(f, 0.0)).astype(BF16)
        acc = acc + jnp.dot(r, wd_ref[c * fc:(c + 1) * fc, :], preferred_element_type=F32)
    y2 = acc.reshape(bb, tm, d)
    y_ref[...] = x1 + gate2 * _rms(y2, gqf_ref[...])


def _out_ffn(od, osb, x, mods, mod_off, g_post_mix, g_pre_ffn, g_post_ffn,
             wo_bf, wu_bf, wd_bf, bb, tm):
    b, s, d = x.shape
    grid = (b // bb, s // tm)
    tok = lambda w: pl.BlockSpec((bb, tm, w), lambda i, t: (i, t, 0))
    const = lambda shape: pl.BlockSpec(shape, lambda i, t: (0,) * len(shape),
                                       pipeline_mode=pl.Buffered(1))
    return pl.pallas_call(
        _out_kernel,
        grid=grid,
        in_specs=[tok(DIFF_WIDTH), tok(SB_WIDTH), tok(d),
                  pl.BlockSpec((bb, 6, d), lambda i, t: (i + mod_off // bb, 0, 0)),
                  const((1, d)), const((1, d)), const((1, d)),
                  const((MIX_WIDTH, d)), const((d, D_FF)), const((D_FF, d))],
        out_specs=tok(d),
        out_shape=jax.ShapeDtypeStruct((b, s, d), F32),
        compiler_params=pltpu.CompilerParams(dimension_semantics=("parallel", "parallel"),
                                             vmem_limit_bytes=VMEM_LIMIT),
        name="out_ffn",
    )(od, osb, x, mods, g_post_mix, g_pre_ffn, g_post_ffn, wo_bf, wu_bf, wd_bf)


def kernel(x_prompt, x_sample, c_prompt, c_sample, cache_diff_k, cache_diff_v, cache_sb_k, cache_sb_v,
           w_ada, b_ada, g_pre_mix, g_post_mix, w_in, lambda_q1, lambda_k1, lambda_q2, lambda_k2,
           diff_subln_g, w_out, g_pre_ffn, g_post_ffn, w_up, w_down):
    bp, sp, d = x_prompt.shape
    bs, ss, _ = x_sample.shape
    past = cache_diff_k.shape[2]
    l = 0

    c_all = jnp.concatenate([c_prompt, c_sample], axis=0)
    mods, lam_tile = _modulation(c_all, w_ada[l], b_ada[l][None, :],
                                 lambda_q1[l][None, :], lambda_k1[l][None, :],
                                 lambda_q2[l][None, :], lambda_k2[l][None, :])
    mods = mods.reshape(bp + bs, 6, d)
    lam = lam_tile[0, 0:1]
    subln_g = diff_subln_g[l].reshape(SLAB, 1)

    w_in_bf = w_in[l].astype(BF16)
    w_out_bf = w_out[l].astype(BF16)
    w_up_bf = w_up[l].astype(BF16)
    w_down_bf = w_down[l].astype(BF16)
    g1 = g_pre_mix[l][None, :]
    g2 = g_post_mix[l][None, :]
    g3 = g_pre_ffn[l][None, :]
    g4 = g_post_ffn[l][None, :]
    diff_shape = lambda b, s: (1, b, s, DIFF_HEADS, 2 * HEAD_DIM)
    sb_shape = lambda b, s: (1, b, s, SB_HEADS, HEAD_DIM)
    sb_result = lambda a, b, s: jnp.transpose(a.reshape(b, SB_HEADS, HEAD_DIM, s),
                                              (0, 3, 1, 2)).reshape(sb_shape(b, s))

    q, kb, vb, kd, vd, ks, vs = _in_proj(x_prompt, mods, 0, g1, w_in_bf, bb=1, tm=512,
                                         sb_feature_major=True)
    od, osb = _prompt_attention(q, kb, vb, lam, subln_g, tq=256)
    y_prompt = _out_ffn(od, osb, x_prompt, mods, 0, g2, g3, g4,
                        w_out_bf, w_up_bf, w_down_bf, bb=1, tm=512)
    prompt_kv = (kd.reshape(diff_shape(bp, sp)), vd.reshape(diff_shape(bp, sp)),
                 sb_result(ks, bp, sp), sb_result(vs, bp, sp))

    q2, kb2, vb2, kd2, vd2, ks2, vs2 = _in_proj(x_sample, mods, bp, g1, w_in_bf, bb=8, tm=ss,
                                                sb_feature_major=False)
    cdk = cache_diff_k[l].reshape(bs, past * DIFF_HEADS, SLAB)
    cdv = cache_diff_v[l].reshape(bs, past * DIFF_HEADS, SLAB)
    cskt = jnp.transpose(cache_sb_k[l], (0, 2, 3, 1)).reshape(bs, SB_WIDTH, past)
    csvt = jnp.transpose(cache_sb_v[l], (0, 2, 3, 1)).reshape(bs, SB_WIDTH, past)
    od2, osb2 = _sample_attention(q2, kb2, vb2, cdk, cdv, cskt, csvt, lam, subln_g, tk=256)
    y_sample = _out_ffn(od2, osb2, x_sample, mods, bp, g2, g3, g4,
                        w_out_bf, w_up_bf, w_down_bf, bb=8, tm=ss)
    sample_kv = (kd2.reshape(diff_shape(bs, ss)), vd2.reshape(diff_shape(bs, ss)),
                 ks2.reshape(sb_shape(bs, ss)), vs2.reshape(sb_shape(bs, ss)))

    return (y_prompt, y_sample, *prompt_kv, *sample_kv)
```

```python
import functools
import math

import jax
import jax.numpy as jnp
import numpy as np
from jax import lax
from jax.experimental import pallas as pl
from jax.experimental.pallas import tpu as pltpu

D_MODEL = 1024
CHUNK = 64
HEAD_DIM = 64
DIFF_HEADS = 4
SB_HEADS = 8
SLAB = 128
DIFF_WIDTH = DIFF_HEADS * 2 * HEAD_DIM
SB_WIDTH = SB_HEADS * HEAD_DIM
MIX_WIDTH = DIFF_WIDTH + SB_WIDTH
IN_WIDTH = 3 * MIX_WIDTH
D_FF = 4 * D_MODEL
EPS = 1e-6
NEG = -1e30
LAMBDA_INIT = 0.8 - 0.6 * math.exp(-0.3 * 0)
ATTN_SCALE = HEAD_DIM ** -0.5
LOG2E = math.log2(math.e)
Q_SCALE = ATTN_SCALE * LOG2E
DEAD_LOG2 = 160.0
ALIBI_SLOPES = tuple(float(2.0 ** (-8.0 * (i + 1) / DIFF_HEADS)) for i in range(DIFF_HEADS))

VMEM_LIMIT = 56 * 1024 * 1024
BF16 = jnp.bfloat16
F32 = jnp.float32

NT_DIMS = (((1,), (1,)), ((), ()))
TN_DIMS = (((0,), (0,)), ((), ()))


def _rms(x, g):
    ms = jnp.mean(x * x, axis=-1, keepdims=True)
    return x * lax.rsqrt(ms + EPS) * g


def _slab(i):
    return slice(i * SLAB, (i + 1) * SLAB)


def _mod_kernel(c_ref, w_ref, b_ref, lq1_ref, lk1_ref, lq2_ref, lk2_ref, m_ref, lam_ref):
    c = c_ref[...]
    s = c * jax.nn.sigmoid(c)
    m_ref[...] = jnp.dot(s, w_ref[...], preferred_element_type=F32,
                         precision=lax.Precision.HIGHEST) + b_ref[...]
    d1 = jnp.sum(lq1_ref[...] * lk1_ref[...], axis=-1, keepdims=True)
    d2 = jnp.sum(lq2_ref[...] * lk2_ref[...], axis=-1, keepdims=True)
    lam = jnp.exp(d1) - jnp.exp(d2) + LAMBDA_INIT
    lam_ref[...] = jnp.broadcast_to(lam, lam_ref.shape)


def _modulation(c_all, w_ada, b_ada, lq1, lk1, lq2, lk2):
    nb = c_all.shape[0]
    tn = 1024
    vec = pl.BlockSpec((1, HEAD_DIM), lambda j: (0, 0))
    return pl.pallas_call(
        _mod_kernel,
        grid=(6 * D_MODEL // tn,),
        in_specs=[pl.BlockSpec((nb, D_MODEL), lambda j: (0, 0)),
                  pl.BlockSpec((D_MODEL, tn), lambda j: (0, j)),
                  pl.BlockSpec((1, tn), lambda j: (0, j)),
                  vec, vec, vec, vec],
        out_specs=[pl.BlockSpec((nb, tn), lambda j: (0, j)),
                   pl.BlockSpec((8, 128), lambda j: (0, 0))],
        out_shape=[jax.ShapeDtypeStruct((nb, 6 * D_MODEL), F32),
                   jax.ShapeDtypeStruct((8, 128), F32)],
        compiler_params=pltpu.CompilerParams(dimension_semantics=("arbitrary",),
                                             vmem_limit_bytes=VMEM_LIMIT),
        name="modulation",
    )(c_all, w_ada, b_ada, lq1, lk1, lq2, lk2)


def _in_kernel(x_ref, mod_ref, g_ref, w_ref,
               q_ref, kb_ref, vb_ref, kd_ref, vd_ref, ks_ref, vs_ref, *, sb_feature_major):
    bb, tm, d = x_ref.shape
    x = x_ref[...]
    shift = mod_ref[:, 0:1, :]
    scale = mod_ref[:, 1:2, :]
    h = _rms(x, g_ref[...]) * (1.0 + scale) + shift
    hb = h.reshape(bb * tm, d).astype(BF16)

    def proj(c):
        u = jnp.dot(hb, w_ref[:, c * 512:(c + 1) * 512], preferred_element_type=F32)
        return u.reshape(bb, tm, 512)

    q_ref[:, :, 0:512] = (proj(0) * Q_SCALE).astype(BF16)
    q_ref[:, :, 512:1024] = (proj(3) * Q_SCALE).astype(BF16)
    for c, f_ref, b_ref in ((1, kd_ref, kb_ref), (2, vd_ref, vb_ref)):
        u = proj(c)
        b_ref[:, :, 0:512] = u.astype(BF16)
        for hd in range(DIFF_HEADS):
            f_ref[:, pl.ds(hd, tm, stride=DIFF_HEADS), :] = u[:, :, _slab(hd)]
    for c, f_ref, b_ref in ((4, ks_ref, kb_ref), (5, vs_ref, vb_ref)):
        u = proj(c)
        b_ref[:, :, 512:1024] = u.astype(BF16)
        if sb_feature_major:
            for i in range(bb):
                f_ref[i] = u[i].T
        else:
            f_ref[...] = u


def _in_proj(x, mods, mod_off, g, w_bf, bb, tm, sb_feature_major):
    b, s, d = x.shape
    grid = (b // bb, s // tm)
    tok = lambda w: pl.BlockSpec((bb, tm, w), lambda i, t: (i, t, 0))
    rows = pl.BlockSpec((bb, tm * DIFF_HEADS, SLAB), lambda i, t: (i, t, 0))
    const = lambda shape: pl.BlockSpec(shape, lambda i, t: (0,) * len(shape),
                                       pipeline_mode=pl.Buffered(1))
    outs = [jax.ShapeDtypeStruct((b, s, MIX_WIDTH), BF16)] * 3 + \
           [jax.ShapeDtypeStruct((b, s * DIFF_HEADS, SLAB), F32)] * 2 + \
           [jax.ShapeDtypeStruct((b, SB_WIDTH, s) if sb_feature_major else (b, s, SB_WIDTH), F32)] * 2
    sb_spec = (pl.BlockSpec((bb, SB_WIDTH, tm), lambda i, t: (i, 0, t)) if sb_feature_major
               else tok(SB_WIDTH))
    return pl.pallas_call(
        functools.partial(_in_kernel, sb_feature_major=sb_feature_major),
        grid=grid,
        in_specs=[tok(d),
                  pl.BlockSpec((bb, 6, d), lambda i, t: (i + mod_off // bb, 0, 0)),
                  const((1, d)),
                  const((d, IN_WIDTH))],
        out_specs=[tok(MIX_WIDTH)] * 3 + [rows] * 2 + [sb_spec] * 2,
        out_shape=outs,
        compiler_params=pltpu.CompilerParams(dimension_semantics=("parallel", "parallel"),
                                             vmem_limit_bytes=VMEM_LIMIT),
        name="in_proj",
    )(x, mods, g, w_bf)


def _half_masks(q):
    lane = lax.broadcasted_iota(jnp.int32, q.shape, 1)
    zero = jnp.zeros_like(q)
    return jnp.where(lane < HEAD_DIM, q, zero), jnp.where(lane >= HEAD_DIM, q, zero)


def _staging(refs, slot):
    def sink(idx, *vals):
        for r, v in zip(refs, vals):
            r[slot, idx] = v

    def source(idx, k):
        return refs[k][slot, idx]
    return sink, source


def _list_staging():
    store = {}

    def sink(idx, *vals):
        store[idx] = vals

    def source(idx, k):
        return store[idx][k]
    return sink, source


def _run_pipeline(stages, stage_refs, *, n_main, diag_in_main):
    front_mm, front_vpu, back_mm, back_vpu = stages

    def body(i, scalars):
        sink, _ = _staging(stage_refs, (i + 1) % 2)
        _, src = _staging(stage_refs, i % 2)
        j_next = n_main - 1 - i
        pre = back_mm(src, False)
        back_vpu(src, pre, scalars, False)
        zs = front_mm(j_next)
        return front_vpu(zs, j_next, sink)

    if diag_in_main:
        sink, _ = _staging(stage_refs, 0)
        scalars = front_vpu(front_mm(None), None, sink)
        first = 0
    else:
        sink, src = _list_staging()
        scalars = front_vpu(front_mm(None), None, sink)
        back_vpu(src, back_mm(src, True), scalars, True)
        sink, _ = _staging(stage_refs, 1)
        scalars = front_vpu(front_mm(n_main - 1), n_main - 1, sink)
        first = 1

    scalars = lax.fori_loop(first, n_main, body, scalars)
    _, src = _staging(stage_refs, n_main % 2)
    back_vpu(src, back_mm(src, False), scalars, False)


def _walk_until_finished(stages, stage_refs, *, n_main, finished):
    front_mm, front_vpu, back_mm, back_vpu = stages
    sink, src = _staging(stage_refs, 0)

    def tile(j):
        scalars = front_vpu(front_mm(j), j, sink)
        back_vpu(src, back_mm(src, False), scalars, False)

    def w_cond(carry):
        i, done = carry
        return jnp.logical_and(i < n_main, jnp.logical_not(done))

    def w_body(carry):
        i, _ = carry
        tile(n_main - 1 - i)
        return i + 1, finished()

    tile(None)
    lax.while_loop(w_cond, w_body, (jnp.int32(0), finished()))


def _diff_kernel(*refs, tk, td, n_main, q_base, diag_base, diag_in_main, stacked, interleaved):
    if diag_in_main:
        (lam_ref, q_ref, km_ref, vm_ref, g_ref, bm_ref, bd_ref, o_ref,
         s_ref, mt_ref, m_ref, l_ref, acc_ref) = refs
        kd_ref = vd_ref = None
    else:
        (lam_ref, q_ref, km_ref, vm_ref, kd_ref, vd_ref, g_ref, bm_ref, bd_ref, o_ref,
         s_ref, mt_ref, m_ref, l_ref, acc_ref) = refs
    tq = q_ref.shape[1]
    nmap = 1 if stacked else 2
    nv = 2 * tq if stacked else tq
    qi = pl.program_id(1)
    lam = lam_ref[0]
    q0 = q_base(qi)
    nm = n_main(qi)
    heads = range(DIFF_HEADS)
    slopes = [s * LOG2E for s in ALIBI_SLOPES]
    qs = []
    for h in heads:
        q1, q2 = _half_masks(q_ref[0, :, _slab(h)])
        qs.append([jnp.concatenate([q1, q2], axis=0)] if stacked else [q1, q2])

    def main_block(ref, j, h):
        if interleaved:
            start = pl.multiple_of(j * (tk * DIFF_HEADS), tk * DIFF_HEADS) + h
            return ref[0, pl.ds(start, tk, stride=DIFF_HEADS), :].astype(BF16)
        return ref[0, pl.ds(pl.multiple_of(j * tk, tk), tk), _slab(h)].astype(BF16)

    def front_mm(j):
        zs = []
        for h in heads:
            if j is not None:
                kblk = main_block(km_ref, j, h)
            elif diag_in_main:
                kblk = main_block(km_ref, qi, h)
            else:
                kblk = kd_ref[0, :, _slab(h)].astype(BF16)
            for qm in qs[h]:
                zs.append(lax.dot_general(kblk, qm, NT_DIMS, preferred_element_type=F32))
        return zs

    def front_vpu(zs, j, sink, items=range(nmap * DIFF_HEADS)):
        bias_ref = bd_ref if j is None else bm_ref
        for idx in items:
            s = zs[idx] + bias_ref[idx // nmap]
            sink(idx, s, jnp.max(s, axis=0, keepdims=True))
        if j is None:
            return jnp.zeros((), F32), jnp.asarray(qi, jnp.int32)
        return jnp.asarray(q0 - j * tk, F32), jnp.asarray(j, jnp.int32)

    def back_mm(source, diag_ref):
        return None

    def back_vpu(source, pre, scalars, diag_ref, items=range(nmap * DIFF_HEADS)):
        dist0, j = scalars
        for idx in items:
            h = idx // nmap
            vblk = vd_ref[0, :, _slab(h)].astype(BF16) if diag_ref else main_block(vm_ref, j, h)
            off = slopes[h] * dist0
            m = m_ref[idx]
            m_new = jnp.maximum(m, source(idx, 1) - off)
            p = jnp.exp2(source(idx, 0) - (m_new + off))
            alpha = jnp.exp2(m - m_new)
            m_ref[idx] = m_new
            l_ref[idx] = alpha * l_ref[idx] + jnp.sum(p, axis=0, keepdims=True)
            pv = lax.dot_general(vblk, p.astype(BF16), TN_DIMS, preferred_element_type=F32)
            acc_ref[idx] = alpha * acc_ref[idx] + pv

    m_ref[...] = jnp.full(m_ref.shape, NEG, F32)
    l_ref[...] = jnp.zeros(l_ref.shape, F32)
    acc_ref[...] = jnp.zeros(acc_ref.shape, F32)
    _run_pipeline((front_mm, front_vpu, back_mm, back_vpu), (s_ref, mt_ref),
                  n_main=nm, diag_in_main=diag_in_main)

    for h in heads:
        if stacked:
            o = acc_ref[h] * (1.0 / l_ref[h])
            o = o[:, 0:tq] - lam * o[:, tq:nv]
        else:
            o = (acc_ref[2 * h] * (1.0 / l_ref[2 * h])
                 - acc_ref[2 * h + 1] * (lam / l_ref[2 * h + 1]))
        ms = jnp.mean(o * o, axis=0, keepdims=True)
        o = o * lax.rsqrt(ms + EPS) * g_ref[...] * (1.0 - LAMBDA_INIT)
        o_ref[0, :, _slab(h)] = o.T.astype(o_ref.dtype)


def _softplus2(z):
    return jnp.maximum(z, jnp.log(1.0 + jnp.exp2(jnp.minimum(z, 126.0))) * LOG2E)


def _sb_kernel(q_ref, km_ref, vm_ref, um_ref, o_ref, zt_ref, tb_ref, t0_ref, carry_ref, acc_ref,
               *, tk):
    tq = q_ref.shape[1]
    qi = pl.program_id(1)
    slabs = range(SB_HEADS // 2)
    qs = [_half_masks(q_ref[0, :, _slab(s)]) for s in slabs]

    def block(ref, j, s):
        return ref[0, pl.ds(pl.multiple_of(j * tk, tk), tk), _slab(s)]

    def front_mm(j):
        zs = []
        for s in slabs:
            kblk = block(km_ref, qi if j is None else j, s)
            for qm in qs[s]:
                zs.append(lax.dot_general(kblk, qm, NT_DIMS, preferred_element_type=F32))
        return zs

    def front_vpu(zs, j, sink, items=range(SB_HEADS)):
        earlier = None
        if j is None:
            earlier = (lax.broadcasted_iota(jnp.int32, (tk, tq), 0)
                       < lax.broadcasted_iota(jnp.int32, (tk, tq), 1))
        for idx in items:
            z = zs[idx]
            t = _softplus2(z)
            zt = z - t
            if earlier is not None:
                t = jnp.where(earlier, t, 0.0)
                zt = jnp.where(earlier, zt, NEG)
            tb = t.astype(BF16)
            sink(idx, zt, tb, tb[0:1, :].astype(F32))
        return (jnp.asarray(qi if j is None else j, jnp.int32),)

    def back_mm(source, diag_ref):
        return [jnp.dot(um_ref[...], source(idx, 1), preferred_element_type=F32)
                for idx in range(SB_HEADS)]

    def back_vpu(source, laters, scalars, diag_ref, items=range(SB_HEADS)):
        (j,) = scalars
        for idx in items:
            vblk = block(vm_ref, j, idx // 2)
            carry = carry_ref[idx]
            later = laters[idx]
            w = jnp.exp2(source(idx, 0) - later - carry)
            pv = lax.dot_general(vblk, w.astype(BF16), TN_DIMS, preferred_element_type=F32)
            carry_ref[idx] = carry + later[0:1, :] + source(idx, 2)
            acc_ref[idx] += pv

    carry_ref[...] = jnp.zeros(carry_ref.shape, F32)
    acc_ref[...] = jnp.zeros(acc_ref.shape, F32)
    def finished():
        return jnp.min(carry_ref[...]) >= DEAD_LOG2

    _walk_until_finished((front_mm, front_vpu, back_mm, back_vpu), (zt_ref, tb_ref, t0_ref),
                         n_main=qi, finished=finished)
    row = lax.broadcasted_iota(jnp.int32, (SLAB, tq), 0)
    for s in slabs:
        o = jnp.where(row < HEAD_DIM, acc_ref[2 * s], acc_ref[2 * s + 1])
        o_ref[0, :, _slab(s)] = o.T.astype(o_ref.dtype)


def _sb_cache_kernel(q_ref, kt_ref, vt_ref, kn_ref, vn_ref, um_ref, ud_ref, o_ref, *, tk):
    tq = q_ref.shape[1]
    nchunk = kt_ref.shape[2] // tk
    slabs = range(SB_HEADS // 2)
    qv = []
    for s in slabs:
        qa, qb = _half_masks(q_ref[0, :, _slab(s)])
        qv.append(jnp.concatenate([qa, qb], axis=0))

    def chunk(ref, s, j):
        return ref[0, _slab(s), j * tk:(j + 1) * tk].astype(BF16)

    def front(j):
        earlier = None
        if j is None:
            r = lax.broadcasted_iota(jnp.int32, (2 * tq, tq), 0) & (tq - 1)
            earlier = lax.broadcasted_iota(jnp.int32, (2 * tq, tq), 1) < r
        zs = []
        for s in slabs:
            if j is None:
                zs.append(lax.dot_general(qv[s], kn_ref[0, :, _slab(s)], NT_DIMS,
                                          preferred_element_type=F32))
            else:
                zs.append(jnp.dot(qv[s], chunk(kt_ref, s, j), preferred_element_type=F32))
        staged = []
        for z in zs:
            t = _softplus2(z)
            zt = z - t
            if earlier is not None:
                t = jnp.where(earlier, t, 0.0)
                zt = jnp.where(earlier, zt, NEG)
            staged.append((zt, t.astype(BF16)))
        return staged

    def back(staged, j, state):
        u = ud_ref[...] if j is None else um_ref[...]
        tb_all = jnp.concatenate([tb for _, tb in staged], axis=0)
        later_all = jnp.dot(tb_all, u, preferred_element_type=F32)
        new_state = []
        for s in slabs:
            zt, tb = staged[s]
            carry, acc = state[s]
            later = later_all[s * 2 * tq:(s + 1) * 2 * tq]
            w = jnp.exp2(zt - later - carry).astype(BF16)
            if j is None:
                pv = jnp.dot(w, vn_ref[0, :, _slab(s)], preferred_element_type=F32)
            else:
                pv = lax.dot_general(w, chunk(vt_ref, s, j), NT_DIMS, preferred_element_type=F32)
            total = later[:, 0:1] + tb[:, 0:1].astype(F32)
            new_state.append((carry + total, acc + pv))
        return new_state

    state = [(jnp.zeros((2 * tq, 1), F32), jnp.zeros((2 * tq, SLAB), F32)) for _ in slabs]
    order = [None] + list(range(nchunk - 1, -1, -1))
    staged = front(order[0])
    for prev, nxt in zip(order[:-1], order[1:]):
        new = front(nxt)
        state = back(staged, prev, state)
        staged = new
    state = back(staged, order[-1], state)

    lane = lax.broadcasted_iota(jnp.int32, (tq, SLAB), 1)
    for s in slabs:
        acc = state[s][1]
        o_ref[0, :, _slab(s)] = jnp.where(lane < HEAD_DIM, acc[0:tq], acc[tq:2 * tq]).astype(o_ref.dtype)


def _later_keys(n, keys_on_rows):
    r = np.arange(n)
    u = (r[None, :] > r[:, None]) if keys_on_rows else (r[:, None] > r[None, :])
    return jnp.asarray(u.astype(np.float32), dtype=BF16)


def _alibi_tables(tq, tk, td, stacked):
    nv = 2 * tq if stacked else tq
    qc = np.arange(nv) % tq
    slopes = np.asarray(ALIBI_SLOPES, np.float64)[:, None, None] * LOG2E
    main = -slopes * (qc[None, :] - np.arange(tk)[:, None])[None]
    r = np.arange(td)[:, None]
    diag = np.where((r // CHUNK) <= (qc[None, :] // CHUNK), -slopes * np.abs(qc[None, :] - r)[None], NEG)
    return jnp.asarray(main, F32), jnp.asarray(diag, F32)


def _prompt_attention(q, kb, vb, lam, subln_g, *, tq):
    b, t, _ = q.shape
    grid = (b, t // tq)
    cparams = pltpu.CompilerParams(dimension_semantics=("parallel", "arbitrary"),
                                   vmem_limit_bytes=VMEM_LIMIT)
    smem = pl.BlockSpec(memory_space=pltpu.SMEM)
    g_spec = pl.BlockSpec((SLAB, 1), lambda i, j: (0, 0))
    out_spec = pl.BlockSpec((1, tq, DIFF_WIDTH), lambda i, j: (i, j, 0))
    out_shape = jax.ShapeDtypeStruct((b, t, DIFF_WIDTH), BF16)

    def specs(half):
        return (pl.BlockSpec((1, tq, DIFF_WIDTH), lambda i, j: (i, j, half)),
                pl.BlockSpec((1, t, DIFF_WIDTH), lambda i, j: (i, 0, half)))

    q_spec, kv_spec = specs(0)
    tab_spec = pl.BlockSpec((DIFF_HEADS, tq, tq), lambda i, j: (0, 0, 0))
    od = pl.pallas_call(
        functools.partial(_diff_kernel, tk=tq, td=tq, n_main=lambda qi: qi,
                          q_base=lambda qi: qi * tq, diag_base=lambda qi: qi * tq,
                          diag_in_main=True, stacked=False, interleaved=False),
        grid=grid, in_specs=[smem, q_spec, kv_spec, kv_spec, g_spec, tab_spec, tab_spec],
        out_specs=out_spec, out_shape=out_shape,
        scratch_shapes=[pltpu.VMEM((2, 2 * DIFF_HEADS, tq, tq), F32),
                        pltpu.VMEM((2, 2 * DIFF_HEADS, 1, tq), F32),
                        pltpu.VMEM((2 * DIFF_HEADS, 1, tq), F32),
                        pltpu.VMEM((2 * DIFF_HEADS, 1, tq), F32),
                        pltpu.VMEM((2 * DIFF_HEADS, SLAB, tq), F32)],
        compiler_params=cparams, name="diff_attention",
    )(lam, q, kb, vb, subln_g, *_alibi_tables(tq, tq, tq, False))

    q_spec, kv_spec = specs(1)
    osb = pl.pallas_call(
        functools.partial(_sb_kernel, tk=tq),
        grid=grid,
        in_specs=[q_spec, kv_spec, kv_spec, pl.BlockSpec((tq, tq), lambda i, j: (0, 0))],
        out_specs=out_spec, out_shape=out_shape,
        scratch_shapes=[pltpu.VMEM((2, SB_HEADS, tq, tq), F32),
                        pltpu.VMEM((2, SB_HEADS, tq, tq), BF16),
                        pltpu.VMEM((2, SB_HEADS, 1, tq), F32),
                        pltpu.VMEM((SB_HEADS, 1, tq), F32),
                        pltpu.VMEM((SB_HEADS, SLAB, tq), F32)],
        compiler_params=cparams, name="sb_attention",
    )(q, kb, vb, _later_keys(tq, True))
    return od, osb


def _sample_attention(q, kb, vb, cdk, cdv, cskt, csvt, lam, subln_g, *, tk):
    b, t, _ = q.shape
    past = cskt.shape[2]
    smem = pl.BlockSpec(memory_space=pltpu.SMEM)
    new = lambda half: pl.BlockSpec((1, t, DIFF_WIDTH), lambda i, j: (i, 0, half))
    out_spec = pl.BlockSpec((1, t, DIFF_WIDTH), lambda i, j: (i, 0, 0))
    out_shape = jax.ShapeDtypeStruct((b, t, DIFF_WIDTH), BF16)
    cparams = pltpu.CompilerParams(dimension_semantics=("parallel", "arbitrary"),
                                   vmem_limit_bytes=VMEM_LIMIT)

    rows = pl.BlockSpec((1, past * DIFF_HEADS, SLAB), lambda i, j: (i, 0, 0))
    od = pl.pallas_call(
        functools.partial(_diff_kernel, tk=tk, td=t, n_main=lambda qi: past // tk,
                          q_base=lambda qi: past, diag_base=lambda qi: past,
                          diag_in_main=False, stacked=True, interleaved=True),
        grid=(b, 1),
        in_specs=[smem, new(0), rows, rows, new(0), new(0),
                  pl.BlockSpec((SLAB, 1), lambda i, j: (0, 0)),
                  pl.BlockSpec((DIFF_HEADS, tk, 2 * t), lambda i, j: (0, 0, 0)),
                  pl.BlockSpec((DIFF_HEADS, t, 2 * t), lambda i, j: (0, 0, 0))],
        out_specs=out_spec, out_shape=out_shape,
        scratch_shapes=[pltpu.VMEM((2, DIFF_HEADS, tk, 2 * t), F32),
                        pltpu.VMEM((2, DIFF_HEADS, 1, 2 * t), F32),
                        pltpu.VMEM((DIFF_HEADS, 1, 2 * t), F32),
                        pltpu.VMEM((DIFF_HEADS, 1, 2 * t), F32),
                        pltpu.VMEM((DIFF_HEADS, SLAB, 2 * t), F32)],
        compiler_params=cparams, name="diff_attention_cache",
    )(lam, q, cdk, cdv, kb, vb, subln_g, *_alibi_tables(t, tk, t, True))

    feat = pl.BlockSpec((1, SB_WIDTH, past), lambda i, j: (i, 0, 0))
    osb = pl.pallas_call(
        functools.partial(_sb_cache_kernel, tk=tk),
        grid=(b, 1),
        in_specs=[new(1), feat, feat, new(1), new(1),
                  pl.BlockSpec((tk, tk), lambda i, j: (0, 0)),
                  pl.BlockSpec((t, t), lambda i, j: (0, 0))],
        out_specs=out_spec, out_shape=out_shape,
        compiler_params=cparams, name="sb_attention_cache",
    )(q, cskt, csvt, kb, vb, _later_keys(tk, False), _later_keys(t, False))
    return od, osb


def _out_kernel(od_ref, osb_ref, x_ref, mod_ref, gpm_ref, gpf_ref, gqf_ref,
                wo_ref, wu_ref, wd_ref, y_ref):
    bb, tm, d = x_ref.shape
    n = bb * tm
    od = od_ref[...].reshape(n, DIFF_WIDTH)
    osb = osb_ref[...].reshape(n, SB_WIDTH)
    y = (jnp.dot(od, wo_ref[0:DIFF_WIDTH, :], preferred_element_type=F32)
         + jnp.dot(osb, wo_ref[DIFF_WIDTH:MIX_WIDTH, :], preferred_element_type=F32))
    y = y.reshape(bb, tm, d)
    gate1 = mod_ref[:, 2:3, :]
    shift2 = mod_ref[:, 3:4, :]
    scale2 = mod_ref[:, 4:5, :]
    gate2 = mod_ref[:, 5:6, :]
    x1 = x_ref[...] + gate1 * _rms(y, gpm_ref[...])
    h2 = _rms(x1, gpf_ref[...]) * (1.0 + scale2) + shift2
    h2b = h2.reshape(n, d).astype(BF16)
    acc = jnp.zeros((n, d), F32)
    fc = 1024
    for c in range(D_FF // fc):
        f = jnp.dot(h2b, wu_ref[:, c * fc:(c + 1) * fc], preferred_element_type=F32)
        r = jnp.square(jnp.maximum(f, 0.0)).astype(BF16)
        acc = acc + jnp.dot(r, wd_ref[c * fc:(c + 1) * fc, :], preferred_element_type=F32)
    y2 = acc.reshape(bb, tm, d)
    y_ref[...] = x1 + gate2 * _rms(y2, gqf_ref[...])


def _out_ffn(od, osb, x, mods, mod_off, g_post_mix, g_pre_ffn, g_post_ffn,
             wo_bf, wu_bf, wd_bf, bb, tm):
    b, s, d = x.shape
    grid = (b // bb, s // tm)
    tok = lambda w: pl.BlockSpec((bb, tm, w), lambda i, t: (i, t, 0))
    const = lambda shape: pl.BlockSpec(shape, lambda i, t: (0,) * len(shape),
                                       pipeline_mode=pl.Buffered(1))
    return pl.pallas_call(
        _out_kernel,
        grid=grid,
        in_specs=[tok(DIFF_WIDTH), tok(SB_WIDTH), tok(d),
                  pl.BlockSpec((bb, 6, d), lambda i, t: (i + mod_off // bb, 0, 0)),
                  const((1, d)), const((1, d)), const((1, d)),
                  const((MIX_WIDTH, d)), const((d, D_FF)), const((D_FF, d))],
        out_specs=tok(d),
        out_shape=jax.ShapeDtypeStruct((b, s, d), F32),
        compiler_params=pltpu.CompilerParams(dimension_semantics=("parallel", "parallel"),
                                             vmem_limit_bytes=VMEM_LIMIT),
        name="out_ffn",
    )(od, osb, x, mods, g_post_mix, g_pre_ffn, g_post_ffn, wo_bf, wu_bf, wd_bf)


def kernel(x_prompt, x_sample, c_prompt, c_sample, cache_diff_k, cache_diff_v, cache_sb_k, cache_sb_v,
           w_ada, b_ada, g_pre_mix, g_post_mix, w_in, lambda_q1, lambda_k1, lambda_q2, lambda_k2,
           diff_subln_g, w_out, g_pre_ffn, g_post_ffn, w_up, w_down):
    bp, sp, d = x_prompt.shape
    bs, ss, _ = x_sample.shape
    past = cache_diff_k.shape[2]
    l = 0

    c_all = jnp.concatenate([c_prompt, c_sample], axis=0)
    mods, lam_tile = _modulation(c_all, w_ada[l], b_ada[l][None, :],
                                 lambda_q1[l][None, :], lambda_k1[l][None, :],
                                 lambda_q2[l][None, :], lambda_k2[l][None, :])
    mods = mods.reshape(bp + bs, 6, d)
    lam = lam_tile[0, 0:1]
    subln_g = diff_subln_g[l].reshape(SLAB, 1)

    w_in_bf = w_in[l].astype(BF16)
    w_out_bf = w_out[l].astype(BF16)
    w_up_bf = w_up[l].astype(BF16)
    w_down_bf = w_down[l].astype(BF16)
    g1 = g_pre_mix[l][None, :]
    g2 = g_post_mix[l][None, :]
    g3 = g_pre_ffn[l][None, :]
    g4 = g_post_ffn[l][None, :]
    diff_shape = lambda b, s: (1, b, s, DIFF_HEADS, 2 * HEAD_DIM)
    sb_shape = lambda b, s: (1, b, s, SB_HEADS, HEAD_DIM)
    sb_result = lambda a, b, s: jnp.transpose(a.reshape(b, SB_HEADS, HEAD_DIM, s),
                                              (0, 3, 1, 2)).reshape(sb_shape(b, s))

    q, kb, vb, kd, vd, ks, vs = _in_proj(x_prompt, mods, 0, g1, w_in_bf, bb=1, tm=512,
                                         sb_feature_major=True)
    od, osb = _prompt_attention(q, kb, vb, lam, subln_g, tq=256)
    y_prompt = _out_ffn(od, osb, x_prompt, mods, 0, g2, g3, g4,
                        w_out_bf, w_up_bf, w_down_bf, bb=1, tm=512)
    prompt_kv = (kd.reshape(diff_shape(bp, sp)), vd.reshape(diff_shape(bp, sp)),
                 sb_result(ks, bp, sp), sb_result(vs, bp, sp))

    q2, kb2, vb2, kd2, vd2, ks2, vs2 = _in_proj(x_sample, mods, bp, g1, w_in_bf, bb=8, tm=ss,
                                                sb_feature_major=False)
    cdk = cache_diff_k[l].reshape(bs, past * DIFF_HEADS, SLAB)
    cdv = cache_diff_v[l].reshape(bs, past * DIFF_HEADS, SLAB)
    cskt = jnp.transpose(cache_sb_k[l], (0, 2, 3, 1)).reshape(bs, SB_WIDTH, past)
    csvt = jnp.transpose(cache_sb_v[l], (0, 2, 3, 1)).reshape(bs, SB_WIDTH, past)
    od2, osb2 = _sample_attention(q2, kb2, vb2, cdk, cdv, cskt, csvt, lam, subln_g, tk=256)
    y_sample = _out_ffn(od2, osb2, x_sample, mods, bp, g2, g3, g4,
                        w_out_bf, w_up_bf, w_down_bf, bb=8, tm=ss)
    sample_kv = (kd2.reshape(diff_shape(bs, ss)), vd2.reshape(diff_shape(bs, ss)),
                 ks2.reshape(sb_shape(bs, ss)), vs2.reshape(sb_shape(bs, ss)))

    return (y_prompt, y_sample, *prompt_kv, *sample_kv)
```

```python
import functools
import math

import jax
import jax.numpy as jnp
import numpy as np
from jax import lax
from jax.experimental import pallas as pl
from jax.experimental.pallas import tpu as pltpu

D_MODEL = 1024
CHUNK = 64
HEAD_DIM = 64
DIFF_HEADS = 4
SB_HEADS = 8
SLAB = 128
DIFF_WIDTH = DIFF_HEADS * 2 * HEAD_DIM
SB_WIDTH = SB_HEADS * HEAD_DIM
MIX_WIDTH = DIFF_WIDTH + SB_WIDTH
IN_WIDTH = 3 * MIX_WIDTH
D_FF = 4 * D_MODEL
EPS = 1e-6
NEG = -1e30
LAMBDA_INIT = 0.8 - 0.6 * math.exp(-0.3 * 0)
ATTN_SCALE = HEAD_DIM ** -0.5
LOG2E = math.log2(math.e)
Q_SCALE = ATTN_SCALE * LOG2E
DEAD_LOG2 = 160.0
ALIBI_SLOPES = tuple(float(2.0 ** (-8.0 * (i + 1) / DIFF_HEADS)) for i in range(DIFF_HEADS))

VMEM_LIMIT = 56 * 1024 * 1024
BF16 = jnp.bfloat16
F32 = jnp.float32

NT_DIMS = (((1,), (1,)), ((), ()))
TN_DIMS = (((0,), (0,)), ((), ()))


def _rms(x, g):
    ms = jnp.mean(x * x, axis=-1, keepdims=True)
    return x * lax.rsqrt(ms + EPS) * g


def _slab(i):
    return slice(i * SLAB, (i + 1) * SLAB)


def _mod_kernel(c_ref, w_ref, b_ref, lq1_ref, lk1_ref, lq2_ref, lk2_ref, m_ref, lam_ref):
    c = c_ref[...]
    s = c * jax.nn.sigmoid(c)
    m_ref[...] = jnp.dot(s, w_ref[...], preferred_element_type=F32,
                         precision=lax.Precision.HIGHEST) + b_ref[...]
    d1 = jnp.sum(lq1_ref[...] * lk1_ref[...], axis=-1, keepdims=True)
    d2 = jnp.sum(lq2_ref[...] * lk2_ref[...], axis=-1, keepdims=True)
    lam = jnp.exp(d1) - jnp.exp(d2) + LAMBDA_INIT
    lam_ref[...] = jnp.broadcast_to(lam, lam_ref.shape)


def _modulation(c_all, w_ada, b_ada, lq1, lk1, lq2, lk2):
    nb = c_all.shape[0]
    tn = 1024
    vec = pl.BlockSpec((1, HEAD_DIM), lambda j: (0, 0))
    return pl.pallas_call(
        _mod_kernel,
        grid=(6 * D_MODEL // tn,),
        in_specs=[pl.BlockSpec((nb, D_MODEL), lambda j: (0, 0)),
                  pl.BlockSpec((D_MODEL, tn), lambda j: (0, j)),
                  pl.BlockSpec((1, tn), lambda j: (0, j)),
                  vec, vec, vec, vec],
        out_specs=[pl.BlockSpec((nb, tn), lambda j: (0, j)),
                   pl.BlockSpec((8, 128), lambda j: (0, 0))],
        out_shape=[jax.ShapeDtypeStruct((nb, 6 * D_MODEL), F32),
                   jax.ShapeDtypeStruct((8, 128), F32)],
        compiler_params=pltpu.CompilerParams(dimension_semantics=("arbitrary",),
                                             vmem_limit_bytes=VMEM_LIMIT),
        name="modulation",
    )(c_all, w_ada, b_ada, lq1, lk1, lq2, lk2)


def _in_kernel(x_ref, mod_ref, g_ref, w_ref,
               q_ref, kb_ref, vb_ref, kd_ref, vd_ref, ks_ref, vs_ref, *, sb_feature_major):
    bb, tm, d = x_ref.shape
    x = x_ref[...]
    shift = mod_ref[:, 0:1, :]
    scale = mod_ref[:, 1:2, :]
    h = _rms(x, g_ref[...]) * (1.0 + scale) + shift
    hb = h.reshape(bb * tm, d).astype(BF16)

    def proj(c):
        u = jnp.dot(hb, w_ref[:, c * 512:(c + 1) * 512], preferred_element_type=F32)
        return u.reshape(bb, tm, 512)

    q_ref[:, :, 0:512] = (proj(0) * Q_SCALE).astype(BF16)
    q_ref[:, :, 512:1024] = (proj(3) * Q_SCALE).astype(BF16)
    for c, f_ref, b_ref in ((1, kd_ref, kb_ref), (2, vd_ref, vb_ref)):
        u = proj(c)
        b_ref[:, :, 0:512] = u.astype(BF16)
        for hd in range(DIFF_HEADS):
            f_ref[:, pl.ds(hd, tm, stride=DIFF_HEADS), :] = u[:, :, _slab(hd)]
    for c, f_ref, b_ref in ((4, ks_ref, kb_ref), (5, vs_ref, vb_ref)):
        u = proj(c)
        b_ref[:, :, 512:1024] = u.astype(BF16)
        if sb_feature_major:
            for i in range(bb):
                f_ref[i] = u[i].T
        else:
            f_ref[...] = u


def _in_proj(x, mods, mod_off, g, w_bf, bb, tm, sb_feature_major):
    b, s, d = x.shape
    grid = (b // bb, s // tm)
    tok = lambda w: pl.BlockSpec((bb, tm, w), lambda i, t: (i, t, 0))
    rows = pl.BlockSpec((bb, tm * DIFF_HEADS, SLAB), lambda i, t: (i, t, 0))
    const = lambda shape: pl.BlockSpec(shape, lambda i, t: (0,) * len(shape),
                                       pipeline_mode=pl.Buffered(1))
    outs = [jax.ShapeDtypeStruct((b, s, MIX_WIDTH), BF16)] * 3 + \
           [jax.ShapeDtypeStruct((b, s * DIFF_HEADS, SLAB), F32)] * 2 + \
           [jax.ShapeDtypeStruct((b, SB_WIDTH, s) if sb_feature_major else (b, s, SB_WIDTH), F32)] * 2
    sb_spec = (pl.BlockSpec((bb, SB_WIDTH, tm), lambda i, t: (i, 0, t)) if sb_feature_major
               else tok(SB_WIDTH))
    return pl.pallas_call(
        functools.partial(_in_kernel, sb_feature_major=sb_feature_major),
        grid=grid,
        in_specs=[tok(d),
                  pl.BlockSpec((bb, 6, d), lambda i, t: (i + mod_off // bb, 0, 0)),
                  const((1, d)),
                  const((d, IN_WIDTH))],
        out_specs=[tok(MIX_WIDTH)] * 3 + [rows] * 2 + [sb_spec] * 2,
        out_shape=outs,
        compiler_params=pltpu.CompilerParams(dimension_semantics=("parallel", "parallel"),
                                             vmem_limit_bytes=VMEM_LIMIT),
        name="in_proj",
    )(x, mods, g, w_bf)


def _half_masks(q):
    lane = lax.broadcasted_iota(jnp.int32, q.shape, 1)
    zero = jnp.zeros_like(q)
    return jnp.where(lane < HEAD_DIM, q, zero), jnp.where(lane >= HEAD_DIM, q, zero)


def _staging(refs, slot):
    def sink(idx, *vals):
        for r, v in zip(refs, vals):
            r[slot, idx] = v

    def source(idx, k):
        return refs[k][slot, idx]
    return sink, source


def _list_staging():
    store = {}

    def sink(idx, *vals):
        store[idx] = vals

    def source(idx, k):
        return store[idx][k]
    return sink, source


def _run_pipeline(stages, stage_refs, *, n_main, diag_in_main):
    front_mm, front_vpu, back_mm, back_vpu = stages

    def body(i, scalars):
        sink, _ = _staging(stage_refs, (i + 1) % 2)
        _, src = _staging(stage_refs, i % 2)
        j_next = n_main - 1 - i
        pre = back_mm(src, False)
        back_vpu(src, pre, scalars, False)
        zs = front_mm(j_next)
        return front_vpu(zs, j_next, sink)

    if diag_in_main:
        sink, _ = _staging(stage_refs, 0)
        scalars = front_vpu(front_mm(None), None, sink)
        first = 0
    else:
        sink, src = _list_staging()
        scalars = front_vpu(front_mm(None), None, sink)
        back_vpu(src, back_mm(src, True), scalars, True)
        sink, _ = _staging(stage_refs, 1)
        scalars = front_vpu(front_mm(n_main - 1), n_main - 1, sink)
        first = 1

    scalars = lax.fori_loop(first, n_main, body, scalars)
    _, src = _staging(stage_refs, n_main % 2)
    back_vpu(src, back_mm(src, False), scalars, False)


def _walk_until_finished(stages, stage_refs, *, n_main, finished):
    front_mm, front_vpu, back_mm, back_vpu = stages
    sink, src = _staging(stage_refs, 0)

    def tile(j):
        scalars = front_vpu(front_mm(j), j, sink)
        back_vpu(src, back_mm(src, False), scalars, False)

    def w_cond(carry):
        i, done = carry
        return jnp.logical_and(i < n_main, jnp.logical_not(done))

    def w_body(carry):
        i, _ = carry
        tile(n_main - 1 - i)
        return i + 1, finished()

    tile(None)
    lax.while_loop(w_cond, w_body, (jnp.int32(0), finished()))


def _diff_kernel(*refs, tk, td, n_main, q_base, diag_base, diag_in_main, stacked, interleaved):
    if diag_in_main:
        (lam_ref, q_ref, km_ref, vm_ref, g_ref, bm_ref, bd_ref, o_ref,
         s_ref, mt_ref, m_ref, l_ref, acc_ref) = refs
        kd_ref = vd_ref = None
    else:
        (lam_ref, q_ref, km_ref, vm_ref, kd_ref, vd_ref, g_ref, bm_ref, bd_ref, o_ref,
         s_ref, mt_ref, m_ref, l_ref, acc_ref) = refs
    tq = q_ref.shape[1]
    nmap = 1 if stacked else 2
    nv = 2 * tq if stacked else tq
    qi = pl.program_id(1)
    lam = lam_ref[0]
    q0 = q_base(qi)
    nm = n_main(qi)
    heads = range(DIFF_HEADS)
    slopes = [s * LOG2E for s in ALIBI_SLOPES]
    qs = []
    for h in heads:
        q1, q2 = _half_masks(q_ref[0, :, _slab(h)])
        qs.append([jnp.concatenate([q1, q2], axis=0)] if stacked else [q1, q2])

    def main_block(ref, j, h):
        if interleaved:
            start = pl.multiple_of(j * (tk * DIFF_HEADS), tk * DIFF_HEADS) + h
            return ref[0, pl.ds(start, tk, stride=DIFF_HEADS), :].astype(BF16)
        return ref[0, pl.ds(pl.multiple_of(j * tk, tk), tk), _slab(h)].astype(BF16)

    def front_mm(j):
        zs = []
        for h in heads:
            if j is not None:
                kblk = main_block(km_ref, j, h)
            elif diag_in_main:
                kblk = main_block(km_ref, qi, h)
            else:
                kblk = kd_ref[0, :, _slab(h)].astype(BF16)
            for qm in qs[h]:
                zs.append(lax.dot_general(kblk, qm, NT_DIMS, preferred_element_type=F32))
        return zs

    def front_vpu(zs, j, sink, items=range(nmap * DIFF_HEADS)):
        bias_ref = bd_ref if j is None else bm_ref
        for idx in items:
            s = zs[idx] + bias_ref[idx // nmap]
            sink(idx, s, jnp.max(s, axis=0, keepdims=True))
        if j is None:
            return jnp.zeros((), F32), jnp.asarray(qi, jnp.int32)
        return jnp.asarray(q0 - j * tk, F32), jnp.asarray(j, jnp.int32)

    def back_mm(source, diag_ref):
        return None

    def back_vpu(source, pre, scalars, diag_ref, items=range(nmap * DIFF_HEADS)):
        dist0, j = scalars
        for idx in items:
            h = idx // nmap
            vblk = vd_ref[0, :, _slab(h)].astype(BF16) if diag_ref else main_block(vm_ref, j, h)
            off = slopes[h] * dist0
            m = m_ref[idx]
            m_new = jnp.maximum(m, source(idx, 1) - off)
            p = jnp.exp2(source(idx, 0) - (m_new + off))
            alpha = jnp.exp2(m - m_new)
            m_ref[idx] = m_new
            l_ref[idx] = alpha * l_ref[idx] + jnp.sum(p, axis=0, keepdims=True)
            pv = lax.dot_general(vblk, p.astype(BF16), TN_DIMS, preferred_element_type=F32)
            acc_ref[idx] = alpha * acc_ref[idx] + pv

    m_ref[...] = jnp.full(m_ref.shape, NEG, F32)
    l_ref[...] = jnp.zeros(l_ref.shape, F32)
    acc_ref[...] = jnp.zeros(acc_ref.shape, F32)
    _run_pipeline((front_mm, front_vpu, back_mm, back_vpu), (s_ref, mt_ref),
                  n_main=nm, diag_in_main=diag_in_main)

    for h in heads:
        if stacked:
            o = acc_ref[h] * (1.0 / l_ref[h])
            o = o[:, 0:tq] - lam * o[:, tq:nv]
        else:
            o = (acc_ref[2 * h] * (1.0 / l_ref[2 * h])
                 - acc_ref[2 * h + 1] * (lam / l_ref[2 * h + 1]))
        ms = jnp.mean(o * o, axis=0, keepdims=True)
        o = o * lax.rsqrt(ms + EPS) * g_ref[...] * (1.0 - LAMBDA_INIT)
        o_ref[0, :, _slab(h)] = o.T.astype(o_ref.dtype)


def _softplus2(z):
    return jnp.maximum(z, jnp.log(1.0 + jnp.exp2(jnp.minimum(z, 126.0))) * LOG2E)


def _sb_kernel(q_ref, km_ref, vm_ref, um_ref, o_ref, zt_ref, tb_ref, t0_ref, carry_ref, acc_ref,
               *, tk):
    tq = q_ref.shape[1]
    qi = pl.program_id(1)
    slabs = range(SB_HEADS // 2)
    qs = [_half_masks(q_ref[0, :, _slab(s)]) for s in slabs]

    def block(ref, j, s):
        return ref[0, pl.ds(pl.multiple_of(j * tk, tk), tk), _slab(s)]

    def front_mm(j):
        zs = []
        for s in slabs:
            kblk = block(km_ref, qi if j is None else j, s)
            for qm in qs[s]:
                zs.append(lax.dot_general(kblk, qm, NT_DIMS, preferred_element_type=F32))
        return zs

    def front_vpu(zs, j, sink, items=range(SB_HEADS)):
        earlier = None
        if j is None:
            earlier = (lax.broadcasted_iota(jnp.int32, (tk, tq), 0)
                       < lax.broadcasted_iota(jnp.int32, (tk, tq), 1))
        for idx in items:
            z = zs[idx]
            t = _softplus2(z)
            zt = z - t
            if earlier is not None:
                t = jnp.where(earlier, t, 0.0)
                zt = jnp.where(earlier, zt, NEG)
            tb = t.astype(BF16)
            sink(idx, zt, tb, tb[0:1, :].astype(F32))
        return (jnp.asarray(qi if j is None else j, jnp.int32),)

    def back_mm(source, diag_ref):
        return [jnp.dot(um_ref[...], source(idx, 1), preferred_element_type=F32)
                for idx in range(SB_HEADS)]

    def back_vpu(source, laters, scalars, diag_ref, items=range(SB_HEADS)):
        (j,) = scalars
        for idx in items:
            vblk = block(vm_ref, j, idx // 2)
            carry = carry_ref[idx]
            later = laters[idx]
            w = jnp.exp2(source(idx, 0) - later - carry)
            pv = lax.dot_general(vblk, w.astype(BF16), TN_DIMS, preferred_element_type=F32)
            carry_ref[idx] = carry + later[0:1, :] + source(idx, 2)
            acc_ref[idx] += pv

    carry_ref[...] = jnp.zeros(carry_ref.shape, F32)
    acc_ref[...] = jnp.zeros(acc_ref.shape, F32)
    def finished():
        return jnp.min(carry_ref[...]) >= DEAD_LOG2

    _walk_until_finished((front_mm, front_vpu, back_mm, back_vpu), (zt_ref, tb_ref, t0_ref),
                         n_main=qi, finished=finished)
    row = lax.broadcasted_iota(jnp.int32, (SLAB, tq), 0)
    for s in slabs:
        o = jnp.where(row < HEAD_DIM, acc_ref[2 * s], acc_ref[2 * s + 1])
        o_ref[0, :, _slab(s)] = o.T.astype(o_ref.dtype)


def _sb_cache_kernel_unrolled(q_ref, kt_ref, vt_ref, kn_ref, vn_ref, um_ref, ud_ref, o_ref, *, tk):
    tq = q_ref.shape[1]
    nchunk = kt_ref.shape[2] // tk
    slabs = range(SB_HEADS // 2)
    qv = []
    for s in slabs:
        qa, qb = _half_masks(q_ref[0, :, _slab(s)])
        qv.append(jnp.concatenate([qa, qb], axis=0))

    def chunk(ref, s, j):
        return ref[0, _slab(s), j * tk:(j + 1) * tk].astype(BF16)

    def front(j):
        earlier = None
        if j is None:
            r = lax.broadcasted_iota(jnp.int32, (2 * tq, tq), 0) & (tq - 1)
            earlier = lax.broadcasted_iota(jnp.int32, (2 * tq, tq), 1) < r
        zs = []
        for s in slabs:
            if j is None:
                zs.append(lax.dot_general(qv[s], kn_ref[0, :, _slab(s)], NT_DIMS,
                                          preferred_element_type=F32))
            else:
                zs.append(jnp.dot(qv[s], chunk(kt_ref, s, j), preferred_element_type=F32))
        staged = []
        for z in zs:
            t = _softplus2(z)
            zt = z - t
            if earlier is not None:
                t = jnp.where(earlier, t, 0.0)
                zt = jnp.where(earlier, zt, NEG)
            staged.append((zt, t.astype(BF16)))
        return staged

    def back(staged, j, state):
        u = ud_ref[...] if j is None else um_ref[...]
        tb_all = jnp.concatenate([tb for _, tb in staged], axis=0)
        later_all = jnp.dot(tb_all, u, preferred_element_type=F32)
        new_state = []
        for s in slabs:
            zt, tb = staged[s]
            carry, acc = state[s]
            later = later_all[s * 2 * tq:(s + 1) * 2 * tq]
            w = jnp.exp2(zt - later - carry).astype(BF16)
            if j is None:
                pv = jnp.dot(w, vn_ref[0, :, _slab(s)], preferred_element_type=F32)
            else:
                pv = lax.dot_general(w, chunk(vt_ref, s, j), NT_DIMS, preferred_element_type=F32)
            total = later[:, 0:1] + tb[:, 0:1].astype(F32)
            new_state.append((carry + total, acc + pv))
        return new_state

    state = [(jnp.zeros((2 * tq, 1), F32), jnp.zeros((2 * tq, SLAB), F32)) for _ in slabs]
    order = [None] + list(range(nchunk - 1, -1, -1))
    staged = front(order[0])
    for prev, nxt in zip(order[:-1], order[1:]):
        new = front(nxt)
        state = back(staged, prev, state)
        staged = new
    state = back(staged, order[-1], state)

    lane = lax.broadcasted_iota(jnp.int32, (tq, SLAB), 1)
    for s in slabs:
        acc = state[s][1]
        o_ref[0, :, _slab(s)] = jnp.where(lane < HEAD_DIM, acc[0:tq], acc[tq:2 * tq]).astype(o_ref.dtype)


def _sb_cache_kernel(q_ref, ktn_ref, vtn_ref, kt_hbm, vt_hbm, kn_ref, vn_ref, um_ref, ud_ref, o_ref,
                     kbuf_ref, vbuf_ref, sem_ref, carry_ref, acc_ref, *, tk):
    tq = q_ref.shape[1]
    nchunk = kt_hbm.shape[2] // tk
    b = pl.program_id(0)
    slabs = range(SB_HEADS // 2)
    qv = []
    for s in slabs:
        qa, qb = _half_masks(q_ref[0, :, _slab(s)])
        qv.append(jnp.concatenate([qa, qb], axis=0))

    def tile(k_of, v_of, u, earlier, new_keys):
        staged = []
        for s in slabs:
            if new_keys:
                z = lax.dot_general(qv[s], k_of(s), NT_DIMS, preferred_element_type=F32)
            else:
                z = jnp.dot(qv[s], k_of(s), preferred_element_type=F32)
            t = _softplus2(z)
            zt = z - t
            if earlier is not None:
                t = jnp.where(earlier, t, 0.0)
                zt = jnp.where(earlier, zt, NEG)
            staged.append((zt, t.astype(BF16)))
        tb_all = jnp.concatenate([tb for _, tb in staged], axis=0)
        later_all = jnp.dot(tb_all, u, preferred_element_type=F32)
        for s in slabs:
            zt, tb = staged[s]
            carry = carry_ref[s]
            later = later_all[s * 2 * tq:(s + 1) * 2 * tq]
            w = jnp.exp2(zt - later - carry).astype(BF16)
            if new_keys:
                pv = jnp.dot(w, v_of(s), preferred_element_type=F32)
            else:
                pv = lax.dot_general(w, v_of(s), NT_DIMS, preferred_element_type=F32)
            carry_ref[s] = carry + later[:, 0:1] + tb[:, 0:1].astype(F32)
            acc_ref[s] += pv

    def finished():
        return jnp.min(carry_ref[...]) >= DEAD_LOG2

    carry_ref[...] = jnp.zeros(carry_ref.shape, F32)
    acc_ref[...] = jnp.zeros(acc_ref.shape, F32)

    r = lax.broadcasted_iota(jnp.int32, (2 * tq, tq), 0) & (tq - 1)
    earlier = lax.broadcasted_iota(jnp.int32, (2 * tq, tq), 1) < r
    tile(lambda s: kn_ref[0, :, _slab(s)], lambda s: vn_ref[0, :, _slab(s)], ud_ref[...], earlier, True)
    tile(lambda s: ktn_ref[0, _slab(s), :].astype(BF16), lambda s: vtn_ref[0, _slab(s), :].astype(BF16),
         um_ref[...], None, False)

    def copies(j):
        lanes = pl.ds(pl.multiple_of(j * tk, tk), tk)
        return (pltpu.make_async_copy(kt_hbm.at[b, :, lanes], kbuf_ref, sem_ref.at[0]),
                pltpu.make_async_copy(vt_hbm.at[b, :, lanes], vbuf_ref, sem_ref.at[1]))

    def w_cond(carry):
        j, done = carry
        return jnp.logical_and(j >= 0, jnp.logical_not(done))

    def w_body(carry):
        j, _ = carry
        for c in copies(j):
            c.start()
        for c in copies(j):
            c.wait()
        tile(lambda s: kbuf_ref[_slab(s), :].astype(BF16), lambda s: vbuf_ref[_slab(s), :].astype(BF16),
             um_ref[...], None, False)
        return j - 1, finished()

    lax.while_loop(w_cond, w_body, (jnp.int32(nchunk - 2), finished()))

    lane = lax.broadcasted_iota(jnp.int32, (tq, SLAB), 1)
    for s in slabs:
        acc = acc_ref[s]
        o_ref[0, :, _slab(s)] = jnp.where(lane < HEAD_DIM, acc[0:tq], acc[tq:2 * tq]).astype(o_ref.dtype)


def _later_keys(n, keys_on_rows):
    r = np.arange(n)
    u = (r[None, :] > r[:, None]) if keys_on_rows else (r[:, None] > r[None, :])
    return jnp.asarray(u.astype(np.float32), dtype=BF16)


def _alibi_tables(tq, tk, td, stacked):
    nv = 2 * tq if stacked else tq
    qc = np.arange(nv) % tq
    slopes = np.asarray(ALIBI_SLOPES, np.float64)[:, None, None] * LOG2E
    main = -slopes * (qc[None, :] - np.arange(tk)[:, None])[None]
    r = np.arange(td)[:, None]
    diag = np.where((r // CHUNK) <= (qc[None, :] // CHUNK), -slopes * np.abs(qc[None, :] - r)[None], NEG)
    return jnp.asarray(main, F32), jnp.asarray(diag, F32)


def _prompt_attention(q, kb, vb, lam, subln_g, *, tq):
    b, t, _ = q.shape
    grid = (b, t // tq)
    cparams = pltpu.CompilerParams(dimension_semantics=("parallel", "arbitrary"),
                                   vmem_limit_bytes=VMEM_LIMIT)
    smem = pl.BlockSpec(memory_space=pltpu.SMEM)
    g_spec = pl.BlockSpec((SLAB, 1), lambda i, j: (0, 0))
    out_spec = pl.BlockSpec((1, tq, DIFF_WIDTH), lambda i, j: (i, j, 0))
    out_shape = jax.ShapeDtypeStruct((b, t, DIFF_WIDTH), BF16)

    def specs(half):
        return (pl.BlockSpec((1, tq, DIFF_WIDTH), lambda i, j: (i, j, half)),
                pl.BlockSpec((1, t, DIFF_WIDTH), lambda i, j: (i, 0, half)))

    q_spec, kv_spec = specs(0)
    tab_spec = pl.BlockSpec((DIFF_HEADS, tq, tq), lambda i, j: (0, 0, 0))
    od = pl.pallas_call(
        functools.partial(_diff_kernel, tk=tq, td=tq, n_main=lambda qi: qi,
                          q_base=lambda qi: qi * tq, diag_base=lambda qi: qi * tq,
                          diag_in_main=True, stacked=False, interleaved=False),
        grid=grid, in_specs=[smem, q_spec, kv_spec, kv_spec, g_spec, tab_spec, tab_spec],
        out_specs=out_spec, out_shape=out_shape,
        scratch_shapes=[pltpu.VMEM((2, 2 * DIFF_HEADS, tq, tq), F32),
                        pltpu.VMEM((2, 2 * DIFF_HEADS, 1, tq), F32),
                        pltpu.VMEM((2 * DIFF_HEADS, 1, tq), F32),
                        pltpu.VMEM((2 * DIFF_HEADS, 1, tq), F32),
                        pltpu.VMEM((2 * DIFF_HEADS, SLAB, tq), F32)],
        compiler_params=cparams, name="diff_attention",
    )(lam, q, kb, vb, subln_g, *_alibi_tables(tq, tq, tq, False))

    q_spec, kv_spec = specs(1)
    osb = pl.pallas_call(
        functools.partial(_sb_kernel, tk=tq),
        grid=grid,
        in_specs=[q_spec, kv_spec, kv_spec, pl.BlockSpec((tq, tq), lambda i, j: (0, 0))],
        out_specs=out_spec, out_shape=out_shape,
        scratch_shapes=[pltpu.VMEM((2, SB_HEADS, tq, tq), F32),
                        pltpu.VMEM((2, SB_HEADS, tq, tq), BF16),
                        pltpu.VMEM((2, SB_HEADS, 1, tq), F32),
                        pltpu.VMEM((SB_HEADS, 1, tq), F32),
                        pltpu.VMEM((SB_HEADS, SLAB, tq), F32)],
        compiler_params=cparams, name="sb_attention",
    )(q, kb, vb, _later_keys(tq, True))
    return od, osb


def _sample_attention(q, kb, vb, cdk, cdv, cskt, csvt, lam, subln_g, *, tk):
    b, t, _ = q.shape
    past = cskt.shape[2]
    smem = pl.BlockSpec(memory_space=pltpu.SMEM)
    new = lambda half: pl.BlockSpec((1, t, DIFF_WIDTH), lambda i, j: (i, 0, half))
    out_spec = pl.BlockSpec((1, t, DIFF_WIDTH), lambda i, j: (i, 0, 0))
    out_shape = jax.ShapeDtypeStruct((b, t, DIFF_WIDTH), BF16)
    cparams = pltpu.CompilerParams(dimension_semantics=("parallel", "arbitrary"),
                                   vmem_limit_bytes=VMEM_LIMIT)

    rows = pl.BlockSpec((1, past * DIFF_HEADS, SLAB), lambda i, j: (i, 0, 0))
    od = pl.pallas_call(
        functools.partial(_diff_kernel, tk=tk, td=t, n_main=lambda qi: past // tk,
                          q_base=lambda qi: past, diag_base=lambda qi: past,
                          diag_in_main=False, stacked=True, interleaved=True),
        grid=(b, 1),
        in_specs=[smem, new(0), rows, rows, new(0), new(0),
                  pl.BlockSpec((SLAB, 1), lambda i, j: (0, 0)),
                  pl.BlockSpec((DIFF_HEADS, tk, 2 * t), lambda i, j: (0, 0, 0)),
                  pl.BlockSpec((DIFF_HEADS, t, 2 * t), lambda i, j: (0, 0, 0))],
        out_specs=out_spec, out_shape=out_shape,
        scratch_shapes=[pltpu.VMEM((2, DIFF_HEADS, tk, 2 * t), F32),
                        pltpu.VMEM((2, DIFF_HEADS, 1, 2 * t), F32),
                        pltpu.VMEM((DIFF_HEADS, 1, 2 * t), F32),
                        pltpu.VMEM((DIFF_HEADS, 1, 2 * t), F32),
                        pltpu.VMEM((DIFF_HEADS, SLAB, 2 * t), F32)],
        compiler_params=cparams, name="diff_attention_cache",
    )(lam, q, cdk, cdv, kb, vb, subln_g, *_alibi_tables(t, tk, t, True))

    newest = pl.BlockSpec((1, SB_WIDTH, tk), lambda i, j: (i, 0, past // tk - 1))
    hbm = pl.BlockSpec(memory_space=pl.ANY)
    osb = pl.pallas_call(
        functools.partial(_sb_cache_kernel, tk=tk),
        grid=(b, 1),
        in_specs=[new(1), newest, newest, hbm, hbm, new(1), new(1),
                  pl.BlockSpec((tk, tk), lambda i, j: (0, 0)),
                  pl.BlockSpec((t, t), lambda i, j: (0, 0))],
        out_specs=out_spec, out_shape=out_shape,
        scratch_shapes=[pltpu.VMEM((SB_WIDTH, tk), F32), pltpu.VMEM((SB_WIDTH, tk), F32),
                        pltpu.SemaphoreType.DMA((2,)),
                        pltpu.VMEM((SB_HEADS // 2, 2 * t, 1), F32),
                        pltpu.VMEM((SB_HEADS // 2, 2 * t, SLAB), F32)],
        compiler_params=cparams, name="sb_attention_cache",
    )(q, cskt, csvt, cskt, csvt, kb, vb, _later_keys(tk, False), _later_keys(t, False))
    return od, osb


def _out_kernel(od_ref, osb_ref, x_ref, mod_ref, gpm_ref, gpf_ref, gqf_ref,
                wo_ref, wu_ref, wd_ref, y_ref):
    bb, tm, d = x_ref.shape
    n = bb * tm
    od = od_ref[...].reshape(n, DIFF_WIDTH)
    osb = osb_ref[...].reshape(n, SB_WIDTH)
    y = (jnp.dot(od, wo_ref[0:DIFF_WIDTH, :], preferred_element_type=F32)
         + jnp.dot(osb, wo_ref[DIFF_WIDTH:MIX_WIDTH, :], preferred_element_type=F32))
    y = y.reshape(bb, tm, d)
    gate1 = mod_ref[:, 2:3, :]
    shift2 = mod_ref[:, 3:4, :]
    scale2 = mod_ref[:, 4:5, :]
    gate2 = mod_ref[:, 5:6, :]
    x1 = x_ref[...] + gate1 * _rms(y, gpm_ref[...])
    h2 = _rms(x1, gpf_ref[...]) * (1.0 + scale2) + shift2
    h2b = h2.reshape(n, d).astype(BF16)
    acc = jnp.zeros((n, d), F32)
    fc = 1024
    for c in range(D_FF // fc):
        f = jnp.dot(h2b, wu_ref[:, c * fc:(c + 1) * fc], preferred_element_type=F32)
        r = jnp.square(jnp.maximum(f, 0.0)).astype(BF16)
        acc = acc + jnp.dot(r, wd_ref[c * fc:(c + 1) * fc, :], preferred_element_type=F32)
    y2 = acc.reshape(bb, tm, d)
    y_ref[...] = x1 + gate2 * _rms(y2, gqf_ref[...])


def _out_ffn(od, osb, x, mods, mod_off, g_post_mix, g_pre_ffn, g_post_ffn,
             wo_bf, wu_bf, wd_bf, bb, tm):
    b, s, d = x.shape
    grid = (b // bb, s // tm)
    tok = lambda w: pl.BlockSpec((bb, tm, w), lambda i, t: (i, t, 0))
    const = lambda shape: pl.BlockSpec(shape, lambda i, t: (0,) * len(shape),
                                       pipeline_mode=pl.Buffered(1))
    return pl.pallas_call(
        _out_kernel,
        grid=grid,
        in_specs=[tok(DIFF_WIDTH), tok(SB_WIDTH), tok(d),
                  pl.BlockSpec((bb, 6, d), lambda i, t: (i + mod_off // bb, 0, 0)),
                  const((1, d)), const((1, d)), const((1, d)),
                  const((MIX_WIDTH, d)), const((d, D_FF)), const((D_FF, d))],
        out_specs=tok(d),
        out_shape=jax.ShapeDtypeStruct((b, s, d), F32),
        compiler_params=pltpu.CompilerParams(dimension_semantics=("parallel", "parallel"),
                                             vmem_limit_bytes=VMEM_LIMIT),
        name="out_ffn",
    )(od, osb, x, mods, g_post_mix, g_pre_ffn, g_post_ffn, wo_bf, wu_bf, wd_bf)


def kernel(x_prompt, x_sample, c_prompt, c_sample, cache_diff_k, cache_diff_v, cache_sb_k, cache_sb_v,
           w_ada, b_ada, g_pre_mix, g_post_mix, w_in, lambda_q1, lambda_k1, lambda_q2, lambda_k2,
           diff_subln_g, w_out, g_pre_ffn, g_post_ffn, w_up, w_down):
    bp, sp, d = x_prompt.shape
    bs, ss, _ = x_sample.shape
    past = cache_diff_k.shape[2]
    l = 0

    c_all = jnp.concatenate([c_prompt, c_sample], axis=0)
    mods, lam_tile = _modulation(c_all, w_ada[l], b_ada[l][None, :],
                                 lambda_q1[l][None, :], lambda_k1[l][None, :],
                                 lambda_q2[l][None, :], lambda_k2[l][None, :])
    mods = mods.reshape(bp + bs, 6, d)
    lam = lam_tile[0, 0:1]
    subln_g = diff_subln_g[l].reshape(SLAB, 1)

    w_in_bf = w_in[l].astype(BF16)
    w_out_bf = w_out[l].astype(BF16)
    w_up_bf = w_up[l].astype(BF16)
    w_down_bf = w_down[l].astype(BF16)
    g1 = g_pre_mix[l][None, :]
    g2 = g_post_mix[l][None, :]
    g3 = g_pre_ffn[l][None, :]
    g4 = g_post_ffn[l][None, :]
    diff_shape = lambda b, s: (1, b, s, DIFF_HEADS, 2 * HEAD_DIM)
    sb_shape = lambda b, s: (1, b, s, SB_HEADS, HEAD_DIM)
    sb_result = lambda a, b, s: jnp.transpose(a.reshape(b, SB_HEADS, HEAD_DIM, s),
                                              (0, 3, 1, 2)).reshape(sb_shape(b, s))

    q, kb, vb, kd, vd, ks, vs = _in_proj(x_prompt, mods, 0, g1, w_in_bf, bb=1, tm=512,
                                         sb_feature_major=True)
    od, osb = _prompt_attention(q, kb, vb, lam, subln_g, tq=256)
    y_prompt = _out_ffn(od, osb, x_prompt, mods, 0, g2, g3, g4,
                        w_out_bf, w_up_bf, w_down_bf, bb=1, tm=512)
    prompt_kv = (kd.reshape(diff_shape(bp, sp)), vd.reshape(diff_shape(bp, sp)),
                 sb_result(ks, bp, sp), sb_result(vs, bp, sp))

    q2, kb2, vb2, kd2, vd2, ks2, vs2 = _in_proj(x_sample, mods, bp, g1, w_in_bf, bb=8, tm=ss,
                                                sb_feature_major=False)
    cdk = cache_diff_k[l].reshape(bs, past * DIFF_HEADS, SLAB)
    cdv = cache_diff_v[l].reshape(bs, past * DIFF_HEADS, SLAB)
    cskt = jnp.transpose(cache_sb_k[l], (0, 2, 3, 1)).reshape(bs, SB_WIDTH, past)
    csvt = jnp.transpose(cache_sb_v[l], (0, 2, 3, 1)).reshape(bs, SB_WIDTH, past)
    od2, osb2 = _sample_attention(q2, kb2, vb2, cdk, cdv, cskt, csvt, lam, subln_g, tk=256)
    y_sample = _out_ffn(od2, osb2, x_sample, mods, bp, g2, g3, g4,
                        w_out_bf, w_up_bf, w_down_bf, bb=8, tm=ss)
    sample_kv = (kd2.reshape(diff_shape(bs, ss)), vd2.reshape(diff_shape(bs, ss)),
                 ks2.reshape(sb_shape(bs, ss)), vs2.reshape(sb_shape(bs, ss)))

    return (y_prompt, y_sample, *prompt_kv, *sample_kv)
```

```python
import functools
import math

import jax
import jax.numpy as jnp
import numpy as np
from jax import lax
from jax.experimental import pallas as pl
from jax.experimental.pallas import tpu as pltpu

D_MODEL = 1024
CHUNK = 64
HEAD_DIM = 64
DIFF_HEADS = 4
SB_HEADS = 8
SLAB = 128
DIFF_WIDTH = DIFF_HEADS * 2 * HEAD_DIM
SB_WIDTH = SB_HEADS * HEAD_DIM
MIX_WIDTH = DIFF_WIDTH + SB_WIDTH
IN_WIDTH = 3 * MIX_WIDTH
D_FF = 4 * D_MODEL
EPS = 1e-6
NEG = -1e30
LAMBDA_INIT = 0.8 - 0.6 * math.exp(-0.3 * 0)
ATTN_SCALE = HEAD_DIM ** -0.5
LOG2E = math.log2(math.e)
Q_SCALE = ATTN_SCALE * LOG2E
OUT_PARTS = 2
DEAD_LOG2 = 160.0
ALIBI_SLOPES = tuple(float(2.0 ** (-8.0 * (i + 1) / DIFF_HEADS)) for i in range(DIFF_HEADS))

VMEM_LIMIT = 56 * 1024 * 1024
BF16 = jnp.bfloat16
F32 = jnp.float32

NT_DIMS = (((1,), (1,)), ((), ()))
TN_DIMS = (((0,), (0,)), ((), ()))


def _rms(x, g):
    ms = jnp.mean(x * x, axis=-1, keepdims=True)
    return x * lax.rsqrt(ms + EPS) * g


def _slab(i):
    return slice(i * SLAB, (i + 1) * SLAB)


def _mod_kernel(c_ref, w_ref, b_ref, lq1_ref, lk1_ref, lq2_ref, lk2_ref, m_ref, lam_ref):
    c = c_ref[...]
    s = c * jax.nn.sigmoid(c)
    m_ref[...] = jnp.dot(s, w_ref[...], preferred_element_type=F32,
                         precision=lax.Precision.HIGHEST) + b_ref[...]
    d1 = jnp.sum(lq1_ref[...] * lk1_ref[...], axis=-1, keepdims=True)
    d2 = jnp.sum(lq2_ref[...] * lk2_ref[...], axis=-1, keepdims=True)
    lam = jnp.exp(d1) - jnp.exp(d2) + LAMBDA_INIT
    lam_ref[...] = jnp.broadcast_to(lam, lam_ref.shape)


def _modulation(c_all, w_ada, b_ada, lq1, lk1, lq2, lk2):
    nb = c_all.shape[0]
    tn = 1024
    vec = pl.BlockSpec((1, HEAD_DIM), lambda j: (0, 0))
    return pl.pallas_call(
        _mod_kernel,
        grid=(6 * D_MODEL // tn,),
        in_specs=[pl.BlockSpec((nb, D_MODEL), lambda j: (0, 0)),
                  pl.BlockSpec((D_MODEL, tn), lambda j: (0, j)),
                  pl.BlockSpec((1, tn), lambda j: (0, j)),
                  vec, vec, vec, vec],
        out_specs=[pl.BlockSpec((nb, tn), lambda j: (0, j)),
                   pl.BlockSpec((8, 128), lambda j: (0, 0))],
        out_shape=[jax.ShapeDtypeStruct((nb, 6 * D_MODEL), F32),
                   jax.ShapeDtypeStruct((8, 128), F32)],
        compiler_params=pltpu.CompilerParams(dimension_semantics=("arbitrary",),
                                             vmem_limit_bytes=VMEM_LIMIT),
        name="modulation",
    )(c_all, w_ada, b_ada, lq1, lk1, lq2, lk2)


def _in_kernel(x_ref, mod_ref, g_ref, w_ref,
               q_ref, kb_ref, vb_ref, kd_ref, vd_ref, ks_ref, vs_ref, *, sb_feature_major):
    bb, tm, d = x_ref.shape
    x = x_ref[...]
    shift = mod_ref[:, 0:1, :]
    scale = mod_ref[:, 1:2, :]
    h = _rms(x, g_ref[...]) * (1.0 + scale) + shift
    hb = h.reshape(bb * tm, d).astype(BF16)

    def proj(c):
        u = jnp.dot(hb, w_ref[:, c * 512:(c + 1) * 512], preferred_element_type=F32)
        return u.reshape(bb, tm, 512)

    q_ref[:, :, 0:512] = (proj(0) * Q_SCALE).astype(BF16)
    q_ref[:, :, 512:1024] = (proj(3) * Q_SCALE).astype(BF16)
    for c, f_ref, b_ref in ((1, kd_ref, kb_ref), (2, vd_ref, vb_ref)):
        u = proj(c)
        b_ref[:, :, 0:512] = u.astype(BF16)
        for hd in range(DIFF_HEADS):
            f_ref[:, pl.ds(hd, tm, stride=DIFF_HEADS), :] = u[:, :, _slab(hd)]
    for c, f_ref, b_ref in ((4, ks_ref, kb_ref), (5, vs_ref, vb_ref)):
        u = proj(c)
        b_ref[:, :, 512:1024] = u.astype(BF16)
        if sb_feature_major:
            for i in range(bb):
                f_ref[i] = u[i].T
        else:
            f_ref[...] = u


def _in_proj(x, mods, mod_off, g, w_bf, bb, tm, sb_feature_major):
    b, s, d = x.shape
    grid = (b // bb, s // tm)
    tok = lambda w: pl.BlockSpec((bb, tm, w), lambda i, t: (i, t, 0))
    rows = pl.BlockSpec((bb, tm * DIFF_HEADS, SLAB), lambda i, t: (i, t, 0))
    const = lambda shape: pl.BlockSpec(shape, lambda i, t: (0,) * len(shape),
                                       pipeline_mode=pl.Buffered(1))
    outs = [jax.ShapeDtypeStruct((b, s, MIX_WIDTH), BF16)] * 3 + \
           [jax.ShapeDtypeStruct((b, s * DIFF_HEADS, SLAB), F32)] * 2 + \
           [jax.ShapeDtypeStruct((b, SB_WIDTH, s) if sb_feature_major else (b, s, SB_WIDTH), F32)] * 2
    sb_spec = (pl.BlockSpec((bb, SB_WIDTH, tm), lambda i, t: (i, 0, t)) if sb_feature_major
               else tok(SB_WIDTH))
    return pl.pallas_call(
        functools.partial(_in_kernel, sb_feature_major=sb_feature_major),
        grid=grid,
        in_specs=[tok(d),
                  pl.BlockSpec((bb, 6, d), lambda i, t: (i + mod_off // bb, 0, 0)),
                  const((1, d)),
                  const((d, IN_WIDTH))],
        out_specs=[tok(MIX_WIDTH)] * 3 + [rows] * 2 + [sb_spec] * 2,
        out_shape=outs,
        compiler_params=pltpu.CompilerParams(dimension_semantics=("parallel", "parallel"),
                                             vmem_limit_bytes=VMEM_LIMIT),
        name="in_proj",
    )(x, mods, g, w_bf)


def _half_masks(q):
    lane = lax.broadcasted_iota(jnp.int32, q.shape, 1)
    zero = jnp.zeros_like(q)
    return jnp.where(lane < HEAD_DIM, q, zero), jnp.where(lane >= HEAD_DIM, q, zero)


def _staging(refs, slot):
    def sink(idx, *vals):
        for r, v in zip(refs, vals):
            r[slot, idx] = v

    def source(idx, k):
        return refs[k][slot, idx]
    return sink, source


def _list_staging():
    store = {}

    def sink(idx, *vals):
        store[idx] = vals

    def source(idx, k):
        return store[idx][k]
    return sink, source


def _run_pipeline(stages, stage_refs, *, n_main, diag_in_main, groups=None):
    front_mm, front_vpu, back_mm, back_vpu = stages

    def body(i, scalars):
        sink, _ = _staging(stage_refs, (i + 1) % 2)
        _, src = _staging(stage_refs, i % 2)
        j_next = n_main - 1 - i
        pre = back_mm(src, False)
        if groups is None:
            back_vpu(src, pre, scalars, False)
            return front_vpu(front_mm(j_next), j_next, sink)
        for items in groups:
            zs = front_mm(j_next, items)
            back_vpu(src, pre, scalars, False, items)
            nxt = front_vpu(zs, j_next, sink, items)
        return nxt

    if diag_in_main:
        sink, _ = _staging(stage_refs, 0)
        scalars = front_vpu(front_mm(None), None, sink)
        first = 0
    else:
        sink, src = _list_staging()
        scalars = front_vpu(front_mm(None), None, sink)
        back_vpu(src, back_mm(src, True), scalars, True)
        sink, _ = _staging(stage_refs, 1)
        scalars = front_vpu(front_mm(n_main - 1), n_main - 1, sink)
        first = 1

    scalars = lax.fori_loop(first, n_main, body, scalars)
    _, src = _staging(stage_refs, n_main % 2)
    back_vpu(src, back_mm(src, False), scalars, False)


def _walk_until_finished(stages, stage_refs, *, n_main, finished):
    front_mm, front_vpu, back_mm, back_vpu = stages

    def tile(j):
        sink, src = _list_staging() if stage_refs is None else _staging(stage_refs, 0)
        scalars = front_vpu(front_mm(j), j, sink)
        back_vpu(src, back_mm(src, False), scalars, False)

    def w_cond(carry):
        i, done = carry
        return jnp.logical_and(i < n_main, jnp.logical_not(done))

    def w_body(carry):
        i, _ = carry
        tile(n_main - 1 - i)
        return i + 1, finished()

    tile(None)
    lax.while_loop(w_cond, w_body, (jnp.int32(0), finished()))


def _diff_kernel(*refs, tk, td, n_main, q_base, diag_base, diag_in_main, stacked, interleaved):
    if diag_in_main:
        (lam_ref, q_ref, km_ref, vm_ref, g_ref, bm_ref, bd_ref, o_ref,
         s_ref, mt_ref, m_ref, l_ref, acc_ref) = refs
        kd_ref = vd_ref = None
    else:
        (lam_ref, q_ref, km_ref, vm_ref, kd_ref, vd_ref, g_ref, bm_ref, bd_ref, o_ref,
         s_ref, mt_ref, m_ref, l_ref, acc_ref) = refs
    tq = q_ref.shape[1]
    nmap = 1 if stacked else 2
    nv = 2 * tq if stacked else tq
    qi = pl.program_id(1)
    lam = lam_ref[0]
    q0 = q_base(qi)
    nm = n_main(qi)
    heads = range(DIFF_HEADS)
    slopes = [s * LOG2E for s in ALIBI_SLOPES]
    qs = []
    for h in heads:
        q1, q2 = _half_masks(q_ref[0, :, _slab(h)])
        qs.append([jnp.concatenate([q1, q2], axis=0)] if stacked else [q1, q2])

    def main_block(ref, j, h):
        if interleaved:
            start = pl.multiple_of(j * (tk * DIFF_HEADS), tk * DIFF_HEADS) + h
            return ref[0, pl.ds(start, tk, stride=DIFF_HEADS), :].astype(BF16)
        return ref[0, pl.ds(pl.multiple_of(j * tk, tk), tk), _slab(h)].astype(BF16)

    def front_mm(j, items=range(nmap * DIFF_HEADS)):
        zs = {}
        for idx in items:
            h = idx // nmap
            if j is not None:
                kblk = main_block(km_ref, j, h)
            elif diag_in_main:
                kblk = main_block(km_ref, qi, h)
            else:
                kblk = kd_ref[0, :, _slab(h)].astype(BF16)
            zs[idx] = lax.dot_general(kblk, qs[h][idx % nmap], NT_DIMS, preferred_element_type=F32)
        return zs

    def front_vpu(zs, j, sink, items=range(nmap * DIFF_HEADS)):
        bias_ref = bd_ref if j is None else bm_ref
        for idx in items:
            s = zs[idx] + bias_ref[idx // nmap]
            sink(idx, s, jnp.max(s, axis=0, keepdims=True))
        if j is None:
            return jnp.zeros((), F32), jnp.asarray(qi, jnp.int32)
        return jnp.asarray(q0 - j * tk, F32), jnp.asarray(j, jnp.int32)

    def back_mm(source, diag_ref):
        return None

    def back_vpu(source, pre, scalars, diag_ref, items=range(nmap * DIFF_HEADS)):
        dist0, j = scalars
        for idx in items:
            h = idx // nmap
            vblk = vd_ref[0, :, _slab(h)].astype(BF16) if diag_ref else main_block(vm_ref, j, h)
            off = slopes[h] * dist0
            m = m_ref[idx]
            m_new = jnp.maximum(m, source(idx, 1) - off)
            p = jnp.exp2(source(idx, 0) - (m_new + off))
            alpha = jnp.exp2(m - m_new)
            m_ref[idx] = m_new
            l_ref[idx] = alpha * l_ref[idx] + jnp.sum(p, axis=0, keepdims=True)
            pv = lax.dot_general(vblk, p.astype(BF16), TN_DIMS, preferred_element_type=F32)
            acc_ref[idx] = alpha * acc_ref[idx] + pv

    m_ref[...] = jnp.full(m_ref.shape, NEG, F32)
    l_ref[...] = jnp.zeros(l_ref.shape, F32)
    acc_ref[...] = jnp.zeros(acc_ref.shape, F32)
    _run_pipeline((front_mm, front_vpu, back_mm, back_vpu), (s_ref, mt_ref),
                  n_main=nm, diag_in_main=diag_in_main)

    for h in heads:
        if stacked:
            o = acc_ref[h] * (1.0 / l_ref[h])
            o = o[:, 0:tq] - lam * o[:, tq:nv]
        else:
            o = (acc_ref[2 * h] * (1.0 / l_ref[2 * h])
                 - acc_ref[2 * h + 1] * (lam / l_ref[2 * h + 1]))
        ms = jnp.mean(o * o, axis=0, keepdims=True)
        o = o * lax.rsqrt(ms + EPS) * g_ref[...] * (1.0 - LAMBDA_INIT)
        o_ref[0, :, _slab(h)] = o.T.astype(o_ref.dtype)


def _softplus2(z):
    return jnp.maximum(z, jnp.log(1.0 + jnp.exp2(jnp.minimum(z, 126.0))) * LOG2E)


def _sb_kernel(q_ref, km_ref, vm_ref, um_ref, o_ref, zt_ref, tb_ref, t0_ref, carry_ref, acc_ref,
               *, tk):
    tq = q_ref.shape[1]
    qi = pl.program_id(1)
    slabs = range(SB_HEADS // 2)
    qs = [_half_masks(q_ref[0, :, _slab(s)]) for s in slabs]

    def block(ref, j, s):
        return ref[0, pl.ds(pl.multiple_of(j * tk, tk), tk), _slab(s)]

    def front_mm(j):
        zs = []
        for s in slabs:
            kblk = block(km_ref, qi if j is None else j, s)
            for qm in qs[s]:
                zs.append(lax.dot_general(kblk, qm, NT_DIMS, preferred_element_type=F32))
        return zs

    def front_vpu(zs, j, sink, items=range(SB_HEADS)):
        earlier = None
        if j is None:
            earlier = (lax.broadcasted_iota(jnp.int32, (tk, tq), 0)
                       < lax.broadcasted_iota(jnp.int32, (tk, tq), 1))
        for idx in items:
            z = zs[idx]
            t = _softplus2(z)
            zt = z - t
            if earlier is not None:
                t = jnp.where(earlier, t, 0.0)
                zt = jnp.where(earlier, zt, NEG)
            tb = t.astype(BF16)
            sink(idx, zt, tb, tb[0:1, :].astype(F32))
        return (jnp.asarray(qi if j is None else j, jnp.int32),)

    def back_mm(source, diag_ref):
        return [jnp.dot(um_ref[...], source(idx, 1), preferred_element_type=F32)
                for idx in range(SB_HEADS)]

    def back_vpu(source, laters, scalars, diag_ref, items=range(SB_HEADS)):
        (j,) = scalars
        for idx in items:
            vblk = block(vm_ref, j, idx // 2)
            carry = carry_ref[idx]
            later = laters[idx]
            w = jnp.exp2(source(idx, 0) - later - carry)
            pv = lax.dot_general(vblk, w.astype(BF16), TN_DIMS, preferred_element_type=F32)
            carry_ref[idx] = carry + later[0:1, :] + source(idx, 2)
            acc_ref[idx] += pv

    carry_ref[...] = jnp.zeros(carry_ref.shape, F32)
    acc_ref[...] = jnp.zeros(acc_ref.shape, F32)
    def finished():
        return jnp.min(carry_ref[...]) >= DEAD_LOG2

    _walk_until_finished((front_mm, front_vpu, back_mm, back_vpu), (zt_ref, tb_ref, t0_ref),
                         n_main=qi, finished=finished)
    row = lax.broadcasted_iota(jnp.int32, (SLAB, tq), 0)
    for s in slabs:
        o = jnp.where(row < HEAD_DIM, acc_ref[2 * s], acc_ref[2 * s + 1])
        o_ref[0, :, _slab(s)] = o.T.astype(o_ref.dtype)


def _sb_cache_kernel_unrolled(q_ref, kt_ref, vt_ref, kn_ref, vn_ref, um_ref, ud_ref, o_ref, *, tk):
    tq = q_ref.shape[1]
    nchunk = kt_ref.shape[2] // tk
    slabs = range(SB_HEADS // 2)
    qv = []
    for s in slabs:
        qa, qb = _half_masks(q_ref[0, :, _slab(s)])
        qv.append(jnp.concatenate([qa, qb], axis=0))

    def chunk(ref, s, j):
        return ref[0, _slab(s), j * tk:(j + 1) * tk].astype(BF16)

    def front(j):
        earlier = None
        if j is None:
            r = lax.broadcasted_iota(jnp.int32, (2 * tq, tq), 0) & (tq - 1)
            earlier = lax.broadcasted_iota(jnp.int32, (2 * tq, tq), 1) < r
        zs = []
        for s in slabs:
            if j is None:
                zs.append(lax.dot_general(qv[s], kn_ref[0, :, _slab(s)], NT_DIMS,
                                          preferred_element_type=F32))
            else:
                zs.append(jnp.dot(qv[s], chunk(kt_ref, s, j), preferred_element_type=F32))
        staged = []
        for z in zs:
            t = _softplus2(z)
            zt = z - t
            if earlier is not None:
                t = jnp.where(earlier, t, 0.0)
                zt = jnp.where(earlier, zt, NEG)
            staged.append((zt, t.astype(BF16)))
        return staged

    def back(staged, j, state):
        u = ud_ref[...] if j is None else um_ref[...]
        tb_all = jnp.concatenate([tb for _, tb in staged], axis=0)
        later_all = jnp.dot(tb_all, u, preferred_element_type=F32)
        new_state = []
        for s in slabs:
            zt, tb = staged[s]
            carry, acc = state[s]
            later = later_all[s * 2 * tq:(s + 1) * 2 * tq]
            w = jnp.exp2(zt - later - carry).astype(BF16)
            if j is None:
                pv = jnp.dot(w, vn_ref[0, :, _slab(s)], preferred_element_type=F32)
            else:
                pv = lax.dot_general(w, chunk(vt_ref, s, j), NT_DIMS, preferred_element_type=F32)
            total = later[:, 0:1] + tb[:, 0:1].astype(F32)
            new_state.append((carry + total, acc + pv))
        return new_state

    state = [(jnp.zeros((2 * tq, 1), F32), jnp.zeros((2 * tq, SLAB), F32)) for _ in slabs]
    order = [None] + list(range(nchunk - 1, -1, -1))
    staged = front(order[0])
    for prev, nxt in zip(order[:-1], order[1:]):
        new = front(nxt)
        state = back(staged, prev, state)
        staged = new
    state = back(staged, order[-1], state)

    lane = lax.broadcasted_iota(jnp.int32, (tq, SLAB), 1)
    for s in slabs:
        acc = state[s][1]
        o_ref[0, :, _slab(s)] = jnp.where(lane < HEAD_DIM, acc[0:tq], acc[tq:2 * tq]).astype(o_ref.dtype)


def _sb_cache_kernel(q_ref, ktn_ref, vtn_ref, kt_hbm, vt_hbm, kn_ref, vn_ref, um_ref, ud_ref, o_ref,
                     kbuf_ref, vbuf_ref, sem_ref, carry_ref, acc_ref, *, tk):
    tq = q_ref.shape[1]
    nchunk = kt_hbm.shape[2] // tk
    b = pl.program_id(0)
    slabs = range(SB_HEADS // 2)
    qv = []
    for s in slabs:
        qa, qb = _half_masks(q_ref[0, :, _slab(s)])
        qv.append(jnp.concatenate([qa, qb], axis=0))

    def tile(k_of, v_of, u, earlier, new_keys):
        staged = []
        for s in slabs:
            if new_keys:
                z = lax.dot_general(qv[s], k_of(s), NT_DIMS, preferred_element_type=F32)
            else:
                z = jnp.dot(qv[s], k_of(s), preferred_element_type=F32)
            t = _softplus2(z)
            zt = z - t
            if earlier is not None:
                t = jnp.where(earlier, t, 0.0)
                zt = jnp.where(earlier, zt, NEG)
            staged.append((zt, t.astype(BF16)))
        tb_all = jnp.concatenate([tb for _, tb in staged], axis=0)
        later_all = jnp.dot(tb_all, u, preferred_element_type=F32)
        for s in slabs:
            zt, tb = staged[s]
            carry = carry_ref[s]
            later = later_all[s * 2 * tq:(s + 1) * 2 * tq]
            w = jnp.exp2(zt - later - carry).astype(BF16)
            if new_keys:
                pv = jnp.dot(w, v_of(s), preferred_element_type=F32)
            else:
                pv = lax.dot_general(w, v_of(s), NT_DIMS, preferred_element_type=F32)
            carry_ref[s] = carry + later[:, 0:1] + tb[:, 0:1].astype(F32)
            acc_ref[s] += pv

    def finished():
        return jnp.min(carry_ref[...]) >= DEAD_LOG2

    carry_ref[...] = jnp.zeros(carry_ref.shape, F32)
    acc_ref[...] = jnp.zeros(acc_ref.shape, F32)

    r = lax.broadcasted_iota(jnp.int32, (2 * tq, tq), 0) & (tq - 1)
    earlier = lax.broadcasted_iota(jnp.int32, (2 * tq, tq), 1) < r
    tile(lambda s: kn_ref[0, :, _slab(s)], lambda s: vn_ref[0, :, _slab(s)], ud_ref[...], earlier, True)
    tile(lambda s: ktn_ref[0, _slab(s), :].astype(BF16), lambda s: vtn_ref[0, _slab(s), :].astype(BF16),
         um_ref[...], None, False)

    def copies(j):
        lanes = pl.ds(pl.multiple_of(j * tk, tk), tk)
        return (pltpu.make_async_copy(kt_hbm.at[b, :, lanes], kbuf_ref, sem_ref.at[0]),
                pltpu.make_async_copy(vt_hbm.at[b, :, lanes], vbuf_ref, sem_ref.at[1]))

    def w_cond(carry):
        j, done = carry
        return jnp.logical_and(j >= 0, jnp.logical_not(done))

    def w_body(carry):
        j, _ = carry
        for c in copies(j):
            c.start()
        for c in copies(j):
            c.wait()
        tile(lambda s: kbuf_ref[_slab(s), :].astype(BF16), lambda s: vbuf_ref[_slab(s), :].astype(BF16),
             um_ref[...], None, False)
        return j - 1, finished()

    lax.while_loop(w_cond, w_body, (jnp.int32(nchunk - 2), finished()))

    lane = lax.broadcasted_iota(jnp.int32, (tq, SLAB), 1)
    for s in slabs:
        acc = acc_ref[s]
        o_ref[0, :, _slab(s)] = jnp.where(lane < HEAD_DIM, acc[0:tq], acc[tq:2 * tq]).astype(o_ref.dtype)


def _later_keys(n, keys_on_rows):
    r = np.arange(n)
    u = (r[None, :] > r[:, None]) if keys_on_rows else (r[:, None] > r[None, :])
    return jnp.asarray(u.astype(np.float32), dtype=BF16)


def _alibi_tables(tq, tk, td, stacked):
    nv = 2 * tq if stacked else tq
    qc = np.arange(nv) % tq
    slopes = np.asarray(ALIBI_SLOPES, np.float64)[:, None, None] * LOG2E
    main = -slopes * (qc[None, :] - np.arange(tk)[:, None])[None]
    r = np.arange(td)[:, None]
    diag = np.where((r // CHUNK) <= (qc[None, :] // CHUNK), -slopes * np.abs(qc[None, :] - r)[None], NEG)
    return jnp.asarray(main, F32), jnp.asarray(diag, F32)


def _prompt_attention(q, kb, vb, lam, subln_g, *, tq):
    b, t, _ = q.shape
    grid = (b, t // tq)
    cparams = pltpu.CompilerParams(dimension_semantics=("parallel", "arbitrary"),
                                   vmem_limit_bytes=VMEM_LIMIT)
    smem = pl.BlockSpec(memory_space=pltpu.SMEM)
    g_spec = pl.BlockSpec((SLAB, 1), lambda i, j: (0, 0))
    out_spec = pl.BlockSpec((1, tq, DIFF_WIDTH), lambda i, j: (i, j, 0))
    out_shape = jax.ShapeDtypeStruct((b, t, DIFF_WIDTH), BF16)

    def specs(half):
        return (pl.BlockSpec((1, tq, DIFF_WIDTH), lambda i, j: (i, j, half)),
                pl.BlockSpec((1, t, DIFF_WIDTH), lambda i, j: (i, 0, half)))

    q_spec, kv_spec = specs(0)
    tab_spec = pl.BlockSpec((DIFF_HEADS, tq, tq), lambda i, j: (0, 0, 0))
    od = pl.pallas_call(
        functools.partial(_diff_kernel, tk=tq, td=tq, n_main=lambda qi: qi,
                          q_base=lambda qi: qi * tq, diag_base=lambda qi: qi * tq,
                          diag_in_main=True, stacked=False, interleaved=False),
        grid=grid, in_specs=[smem, q_spec, kv_spec, kv_spec, g_spec, tab_spec, tab_spec],
        out_specs=out_spec, out_shape=out_shape,
        scratch_shapes=[pltpu.VMEM((2, 2 * DIFF_HEADS, tq, tq), F32),
                        pltpu.VMEM((2, 2 * DIFF_HEADS, 1, tq), F32),
                        pltpu.VMEM((2 * DIFF_HEADS, 1, tq), F32),
                        pltpu.VMEM((2 * DIFF_HEADS, 1, tq), F32),
                        pltpu.VMEM((2 * DIFF_HEADS, SLAB, tq), F32)],
        compiler_params=cparams, name="diff_attention",
    )(lam, q, kb, vb, subln_g, *_alibi_tables(tq, tq, tq, False))

    q_spec, kv_spec = specs(1)
    osb = pl.pallas_call(
        functools.partial(_sb_kernel, tk=tq),
        grid=grid,
        in_specs=[q_spec, kv_spec, kv_spec, pl.BlockSpec((tq, tq), lambda i, j: (0, 0))],
        out_specs=out_spec, out_shape=out_shape,
        scratch_shapes=[pltpu.VMEM((2, SB_HEADS, tq, tq), F32),
                        pltpu.VMEM((2, SB_HEADS, tq, tq), BF16),
                        pltpu.VMEM((2, SB_HEADS, 1, tq), F32),
                        pltpu.VMEM((SB_HEADS, 1, tq), F32),
                        pltpu.VMEM((SB_HEADS, SLAB, tq), F32)],
        compiler_params=cparams, name="sb_attention",
    )(q, kb, vb, _later_keys(tq, True))
    return od, osb


def _sample_attention(q, kb, vb, cdk, cdv, cskt, csvt, lam, subln_g, *, tk):
    b, t, _ = q.shape
    past = cskt.shape[2]
    smem = pl.BlockSpec(memory_space=pltpu.SMEM)
    new = lambda half: pl.BlockSpec((1, t, DIFF_WIDTH), lambda i, j: (i, 0, half))
    out_spec = pl.BlockSpec((1, t, DIFF_WIDTH), lambda i, j: (i, 0, 0))
    out_shape = jax.ShapeDtypeStruct((b, t, DIFF_WIDTH), BF16)
    cparams = pltpu.CompilerParams(dimension_semantics=("parallel", "arbitrary"),
                                   vmem_limit_bytes=VMEM_LIMIT)

    rows = pl.BlockSpec((1, past * DIFF_HEADS, SLAB), lambda i, j: (i, 0, 0))
    od = pl.pallas_call(
        functools.partial(_diff_kernel, tk=tk, td=t, n_main=lambda qi: past // tk,
                          q_base=lambda qi: past, diag_base=lambda qi: past,
                          diag_in_main=False, stacked=True, interleaved=True),
        grid=(b, 1),
        in_specs=[smem, new(0), rows, rows, new(0), new(0),
                  pl.BlockSpec((SLAB, 1), lambda i, j: (0, 0)),
                  pl.BlockSpec((DIFF_HEADS, tk, 2 * t), lambda i, j: (0, 0, 0)),
                  pl.BlockSpec((DIFF_HEADS, t, 2 * t), lambda i, j: (0, 0, 0))],
        out_specs=out_spec, out_shape=out_shape,
        scratch_shapes=[pltpu.VMEM((2, DIFF_HEADS, tk, 2 * t), F32),
                        pltpu.VMEM((2, DIFF_HEADS, 1, 2 * t), F32),
                        pltpu.VMEM((DIFF_HEADS, 1, 2 * t), F32),
                        pltpu.VMEM((DIFF_HEADS, 1, 2 * t), F32),
                        pltpu.VMEM((DIFF_HEADS, SLAB, 2 * t), F32)],
        compiler_params=cparams, name="diff_attention_cache",
    )(lam, q, cdk, cdv, kb, vb, subln_g, *_alibi_tables(t, tk, t, True))

    newest = pl.BlockSpec((1, SB_WIDTH, tk), lambda i, j: (i, 0, past // tk - 1))
    hbm = pl.BlockSpec(memory_space=pl.ANY)
    osb = pl.pallas_call(
        functools.partial(_sb_cache_kernel, tk=tk),
        grid=(b, 1),
        in_specs=[new(1), newest, newest, hbm, hbm, new(1), new(1),
                  pl.BlockSpec((tk, tk), lambda i, j: (0, 0)),
                  pl.BlockSpec((t, t), lambda i, j: (0, 0))],
        out_specs=out_spec, out_shape=out_shape,
        scratch_shapes=[pltpu.VMEM((SB_WIDTH, tk), F32), pltpu.VMEM((SB_WIDTH, tk), F32),
                        pltpu.SemaphoreType.DMA((2,)),
                        pltpu.VMEM((SB_HEADS // 2, 2 * t, 1), F32),
                        pltpu.VMEM((SB_HEADS // 2, 2 * t, SLAB), F32)],
        compiler_params=cparams, name="sb_attention_cache",
    )(q, cskt, csvt, cskt, csvt, kb, vb, _later_keys(tk, False), _later_keys(t, False))
    return od, osb


def _out_kernel(od_ref, osb_ref, x_ref, mod_ref, gpm_ref, gpf_ref, gqf_ref,
                wo_ref, wu_ref, wd_ref, y_ref):
    bb, tm, d = x_ref.shape
    np_ = OUT_PARTS
    if bb > 1:
        parts = [(slice(h * bb // np_, (h + 1) * bb // np_), slice(None)) for h in range(np_)]
    else:
        parts = [(slice(None), slice(h * tm // np_, (h + 1) * tm // np_)) for h in range(np_)]
    n = bb * tm // np_
    fc = 1024

    ys = []
    for bs, ts in parts:
        od = od_ref[bs, ts, :].reshape(n, DIFF_WIDTH)
        osb = osb_ref[bs, ts, :].reshape(n, SB_WIDTH)
        ys.append(jnp.dot(od, wo_ref[0:DIFF_WIDTH, :], preferred_element_type=F32)
                  + jnp.dot(osb, wo_ref[DIFF_WIDTH:MIX_WIDTH, :], preferred_element_type=F32))
    x1s, h2s = [], []
    for (bs, ts), y in zip(parts, ys):
        mod = mod_ref[bs]
        x = x_ref[bs, ts, :]
        x1 = x + mod[:, 2:3, :] * _rms(y.reshape(x.shape), gpm_ref[...])
        h2 = _rms(x1, gpf_ref[...]) * (1.0 + mod[:, 4:5, :]) + mod[:, 3:4, :]
        x1s.append(x1)
        h2s.append(h2.reshape(n, d).astype(BF16))
    accs = []
    for h2b in h2s:
        acc = jnp.zeros((n, d), F32)
        for c in range(D_FF // fc):
            f = jnp.dot(h2b, wu_ref[:, c * fc:(c + 1) * fc], preferred_element_type=F32)
            r = jnp.square(jnp.maximum(f, 0.0)).astype(BF16)
            acc = acc + jnp.dot(r, wd_ref[c * fc:(c + 1) * fc, :], preferred_element_type=F32)
        accs.append(acc)
    for (bs, ts), x1, acc in zip(parts, x1s, accs):
        y_ref[bs, ts, :] = x1 + mod_ref[bs][:, 5:6, :] * _rms(acc.reshape(x1.shape), gqf_ref[...])


def _out_ffn(od, osb, x, mods, mod_off, g_post_mix, g_pre_ffn, g_post_ffn,
             wo_bf, wu_bf, wd_bf, bb, tm):
    b, s, d = x.shape
    grid = (b // bb, s // tm)
    tok = lambda w: pl.BlockSpec((bb, tm, w), lambda i, t: (i, t, 0))
    const = lambda shape: pl.BlockSpec(shape, lambda i, t: (0,) * len(shape),
                                       pipeline_mode=pl.Buffered(1))
    return pl.pallas_call(
        _out_kernel,
        grid=grid,
        in_specs=[tok(DIFF_WIDTH), tok(SB_WIDTH), tok(d),
                  pl.BlockSpec((bb, 6, d), lambda i, t: (i + mod_off // bb, 0, 0)),
                  const((1, d)), const((1, d)), const((1, d)),
                  const((MIX_WIDTH, d)), const((d, D_FF)), const((D_FF, d))],
        out_specs=tok(d),
        out_shape=jax.ShapeDtypeStruct((b, s, d), F32),
        compiler_params=pltpu.CompilerParams(dimension_semantics=("parallel", "parallel"),
                                             vmem_limit_bytes=VMEM_LIMIT),
        name="out_ffn",
    )(od, osb, x, mods, g_post_mix, g_pre_ffn, g_post_ffn, wo_bf, wu_bf, wd_bf)


def kernel(x_prompt, x_sample, c_prompt, c_sample, cache_diff_k, cache_diff_v, cache_sb_k, cache_sb_v,
           w_ada, b_ada, g_pre_mix, g_post_mix, w_in, lambda_q1, lambda_k1, lambda_q2, lambda_k2,
           diff_subln_g, w_out, g_pre_ffn, g_post_ffn, w_up, w_down):
    bp, sp, d = x_prompt.shape
    bs, ss, _ = x_sample.shape
    past = cache_diff_k.shape[2]
    l = 0

    c_all = jnp.concatenate([c_prompt, c_sample], axis=0)
    mods, lam_tile = _modulation(c_all, w_ada[l], b_ada[l][None, :],
                                 lambda_q1[l][None, :], lambda_k1[l][None, :],
                                 lambda_q2[l][None, :], lambda_k2[l][None, :])
    mods = mods.reshape(bp + bs, 6, d)
    lam = lam_tile[0, 0:1]
    subln_g = diff_subln_g[l].reshape(SLAB, 1)

    w_in_bf = w_in[l].astype(BF16)
    w_out_bf = w_out[l].astype(BF16)
    w_up_bf = w_up[l].astype(BF16)
    w_down_bf = w_down[l].astype(BF16)
    g1 = g_pre_mix[l][None, :]
    g2 = g_post_mix[l][None, :]
    g3 = g_pre_ffn[l][None, :]
    g4 = g_post_ffn[l][None, :]
    diff_shape = lambda b, s: (1, b, s, DIFF_HEADS, 2 * HEAD_DIM)
    sb_shape = lambda b, s: (1, b, s, SB_HEADS, HEAD_DIM)
    sb_result = lambda a, b, s: jnp.transpose(a.reshape(b, SB_HEADS, HEAD_DIM, s),
                                              (0, 3, 1, 2)).reshape(sb_shape(b, s))

    q, kb, vb, kd, vd, ks, vs = _in_proj(x_prompt, mods, 0, g1, w_in_bf, bb=1, tm=512,
                                         sb_feature_major=True)
    od, osb = _prompt_attention(q, kb, vb, lam, subln_g, tq=256)
    y_prompt = _out_ffn(od, osb, x_prompt, mods, 0, g2, g3, g4,
                        w_out_bf, w_up_bf, w_down_bf, bb=1, tm=512)
    prompt_kv = (kd.reshape(diff_shape(bp, sp)), vd.reshape(diff_shape(bp, sp)),
                 sb_result(ks, bp, sp), sb_result(vs, bp, sp))

    q2, kb2, vb2, kd2, vd2, ks2, vs2 = _in_proj(x_sample, mods, bp, g1, w_in_bf, bb=8, tm=ss,
                                                sb_feature_major=False)
    cdk = cache_diff_k[l].reshape(bs, past * DIFF_HEADS, SLAB)
    cdv = cache_diff_v[l].reshape(bs, past * DIFF_HEADS, SLAB)
    cskt = jnp.transpose(cache_sb_k[l], (0, 2, 3, 1)).reshape(bs, SB_WIDTH, past)
    csvt = jnp.transpose(cache_sb_v[l], (0, 2, 3, 1)).reshape(bs, SB_WIDTH, past)
    od2, osb2 = _sample_attention(q2, kb2, vb2, cdk, cdv, cskt, csvt, lam, subln_g, tk=256)
    y_sample = _out_ffn(od2, osb2, x_sample, mods, bp, g2, g3, g4,
                        w_out_bf, w_up_bf, w_down_bf, bb=8, tm=ss)
    sample_kv = (kd2.reshape(diff_shape(bs, ss)), vd2.reshape(diff_shape(bs, ss)),
                 ks2.reshape(sb_shape(bs, ss)), vs2.reshape(sb_shape(bs, ss)))

    return (y_prompt, y_sample, *prompt_kv, *sample_kv)
```

```python
import functools
import math

import jax
import jax.numpy as jnp
import numpy as np
from jax import lax
from jax.experimental import pallas as pl
from jax.experimental.pallas import tpu as pltpu

D_MODEL = 1024
CHUNK = 64
HEAD_DIM = 64
DIFF_HEADS = 4
SB_HEADS = 8
SLAB = 128
DIFF_WIDTH = DIFF_HEADS * 2 * HEAD_DIM
SB_WIDTH = SB_HEADS * HEAD_DIM
MIX_WIDTH = DIFF_WIDTH + SB_WIDTH
IN_WIDTH = 3 * MIX_WIDTH
D_FF = 4 * D_MODEL
EPS = 1e-6
NEG = -1e30
LAMBDA_INIT = 0.8 - 0.6 * math.exp(-0.3 * 0)
ATTN_SCALE = HEAD_DIM ** -0.5
LOG2E = math.log2(math.e)
Q_SCALE = ATTN_SCALE * LOG2E
DEAD_LOG2 = 160.0
OUT_PARTS = 2
ALIBI_SLOPES = tuple(float(2.0 ** (-8.0 * (i + 1) / DIFF_HEADS)) for i in range(DIFF_HEADS))

V7X_VMEM_BYTES = 64 * 1024 * 1024
VMEM_LIMIT = V7X_VMEM_BYTES - 8 * 1024 * 1024
BF16 = jnp.bfloat16
F32 = jnp.float32

NT_DIMS = (((1,), (1,)), ((), ()))
TN_DIMS = (((0,), (0,)), ((), ()))


def _rms(x, g):
    ms = jnp.mean(x * x, axis=-1, keepdims=True)
    return x * lax.rsqrt(ms + EPS) * g


def _slab(i):
    return slice(i * SLAB, (i + 1) * SLAB)


def _mod_kernel(c_ref, w_ref, b_ref, lq1_ref, lk1_ref, lq2_ref, lk2_ref, m_ref, lam_ref):
    c = c_ref[...]
    s = c * jax.nn.sigmoid(c)
    m_ref[...] = jnp.dot(s, w_ref[...], preferred_element_type=F32,
                         precision=lax.Precision.HIGHEST) + b_ref[...]
    d1 = jnp.sum(lq1_ref[...] * lk1_ref[...], axis=-1, keepdims=True)
    d2 = jnp.sum(lq2_ref[...] * lk2_ref[...], axis=-1, keepdims=True)
    lam = jnp.exp(d1) - jnp.exp(d2) + LAMBDA_INIT
    lam_ref[...] = jnp.broadcast_to(lam, lam_ref.shape)


def _modulation(c_all, w_ada, b_ada, lq1, lk1, lq2, lk2):
    nb = c_all.shape[0]
    tn = 1024
    vec = pl.BlockSpec((1, HEAD_DIM), lambda j: (0, 0))
    return pl.pallas_call(
        _mod_kernel,
        grid=(6 * D_MODEL // tn,),
        in_specs=[pl.BlockSpec((nb, D_MODEL), lambda j: (0, 0)),
                  pl.BlockSpec((D_MODEL, tn), lambda j: (0, j)),
                  pl.BlockSpec((1, tn), lambda j: (0, j)),
                  vec, vec, vec, vec],
        out_specs=[pl.BlockSpec((nb, tn), lambda j: (0, j)),
                   pl.BlockSpec((8, 128), lambda j: (0, 0))],
        out_shape=[jax.ShapeDtypeStruct((nb, 6 * D_MODEL), F32),
                   jax.ShapeDtypeStruct((8, 128), F32)],
        compiler_params=pltpu.CompilerParams(dimension_semantics=("arbitrary",),
                                             vmem_limit_bytes=VMEM_LIMIT),
        name="modulation",
    )(c_all, w_ada, b_ada, lq1, lk1, lq2, lk2)


def _in_kernel(x_ref, mod_ref, g_ref, w_ref,
               q_ref, kb_ref, vb_ref, kd_ref, vd_ref, ks_ref, vs_ref, *, sb_feature_major):
    bb, tm, d = x_ref.shape
    x = x_ref[...]
    shift = mod_ref[:, 0:1, :]
    scale = mod_ref[:, 1:2, :]
    h = _rms(x, g_ref[...]) * (1.0 + scale) + shift
    hb = h.reshape(bb * tm, d).astype(BF16)

    def proj(c):
        u = jnp.dot(hb, w_ref[:, c * 512:(c + 1) * 512], preferred_element_type=F32)
        return u.reshape(bb, tm, 512)

    q_ref[:, :, 0:512] = (proj(0) * Q_SCALE).astype(BF16)
    q_ref[:, :, 512:1024] = (proj(3) * Q_SCALE).astype(BF16)
    for c, f_ref, b_ref in ((1, kd_ref, kb_ref), (2, vd_ref, vb_ref)):
        u = proj(c)
        b_ref[:, :, 0:512] = u.astype(BF16)
        for hd in range(DIFF_HEADS):
            f_ref[:, pl.ds(hd, tm, stride=DIFF_HEADS), :] = u[:, :, _slab(hd)]
    for c, f_ref, b_ref in ((4, ks_ref, kb_ref), (5, vs_ref, vb_ref)):
        u = proj(c)
        b_ref[:, :, 512:1024] = u.astype(BF16)
        if sb_feature_major:
            for i in range(bb):
                f_ref[i] = u[i].T
        else:
            f_ref[...] = u


def _in_proj(x, mods, mod_off, g, w_bf, bb, tm, sb_feature_major):
    b, s, d = x.shape
    grid = (b // bb, s // tm)
    tok = lambda w: pl.BlockSpec((bb, tm, w), lambda i, t: (i, t, 0))
    rows = pl.BlockSpec((bb, tm * DIFF_HEADS, SLAB), lambda i, t: (i, t, 0))
    const = lambda shape: pl.BlockSpec(shape, lambda i, t: (0,) * len(shape),
                                       pipeline_mode=pl.Buffered(1))
    outs = [jax.ShapeDtypeStruct((b, s, MIX_WIDTH), BF16)] * 3 + \
           [jax.ShapeDtypeStruct((b, s * DIFF_HEADS, SLAB), F32)] * 2 + \
           [jax.ShapeDtypeStruct((b, SB_WIDTH, s) if sb_feature_major else (b, s, SB_WIDTH), F32)] * 2
    sb_spec = (pl.BlockSpec((bb, SB_WIDTH, tm), lambda i, t: (i, 0, t)) if sb_feature_major
               else tok(SB_WIDTH))
    return pl.pallas_call(
        functools.partial(_in_kernel, sb_feature_major=sb_feature_major),
        grid=grid,
        in_specs=[tok(d),
                  pl.BlockSpec((bb, 6, d), lambda i, t: (i + mod_off // bb, 0, 0)),
                  const((1, d)),
                  const((d, IN_WIDTH))],
        out_specs=[tok(MIX_WIDTH)] * 3 + [rows] * 2 + [sb_spec] * 2,
        out_shape=outs,
        compiler_params=pltpu.CompilerParams(dimension_semantics=("parallel", "parallel"),
                                             vmem_limit_bytes=VMEM_LIMIT),
        name="in_proj",
    )(x, mods, g, w_bf)


def _half_masks(q):
    lane = lax.broadcasted_iota(jnp.int32, q.shape, 1)
    zero = jnp.zeros_like(q)
    return jnp.where(lane < HEAD_DIM, q, zero), jnp.where(lane >= HEAD_DIM, q, zero)


def _staging(refs, slot):
    def sink(idx, *vals):
        for r, v in zip(refs, vals):
            r[slot, idx] = v

    def source(idx, k):
        return refs[k][slot, idx]
    return sink, source


def _list_staging():
    store = {}

    def sink(idx, *vals):
        store[idx] = vals

    def source(idx, k):
        return store[idx][k]
    return sink, source


def _run_pipeline(stages, stage_refs, *, n_main, diag_in_main):
    front_mm, front_vpu, back_mm, back_vpu = stages

    def body(i, scalars):
        sink, _ = _staging(stage_refs, (i + 1) % 2)
        _, src = _staging(stage_refs, i % 2)
        j_next = n_main - 1 - i
        pre = back_mm(src, False)
        back_vpu(src, pre, scalars, False)
        return front_vpu(front_mm(j_next), j_next, sink)

    if diag_in_main:
        sink, _ = _staging(stage_refs, 0)
        scalars = front_vpu(front_mm(None), None, sink)
        first = 0
    else:
        sink, src = _list_staging()
        scalars = front_vpu(front_mm(None), None, sink)
        back_vpu(src, back_mm(src, True), scalars, True)
        sink, _ = _staging(stage_refs, 1)
        scalars = front_vpu(front_mm(n_main - 1), n_main - 1, sink)
        first = 1

    scalars = lax.fori_loop(first, n_main, body, scalars)
    _, src = _staging(stage_refs, n_main % 2)
    back_vpu(src, back_mm(src, False), scalars, False)


def _walk_until_finished(stages, stage_refs, *, n_main, finished):
    front_mm, front_vpu, back_mm, back_vpu = stages
    sink, src = _staging(stage_refs, 0)

    def tile(j):
        scalars = front_vpu(front_mm(j), j, sink)
        back_vpu(src, back_mm(src, False), scalars, False)

    def w_cond(carry):
        i, done = carry
        return jnp.logical_and(i < n_main, jnp.logical_not(done))

    def w_body(carry):
        i, _ = carry
        tile(n_main - 1 - i)
        return i + 1, finished()

    tile(None)
    lax.while_loop(w_cond, w_body, (jnp.int32(0), finished()))


def _diff_kernel(*refs, tk, td, n_main, q_base, diag_in_main, stacked, interleaved):
    if diag_in_main:
        (lam_ref, q_ref, km_ref, vm_ref, g_ref, bm_ref, bd_ref, o_ref,
         s_ref, mt_ref, m_ref, l_ref, acc_ref) = refs
        kd_ref = vd_ref = None
    else:
        (lam_ref, q_ref, km_ref, vm_ref, kd_ref, vd_ref, g_ref, bm_ref, bd_ref, o_ref,
         s_ref, mt_ref, m_ref, l_ref, acc_ref) = refs
    tq = q_ref.shape[1]
    nmap = 1 if stacked else 2
    nv = 2 * tq if stacked else tq
    n_items = nmap * DIFF_HEADS
    qi = pl.program_id(1)
    lam = lam_ref[0]
    q0 = q_base(qi)
    slopes = [s * LOG2E for s in ALIBI_SLOPES]
    qs = []
    for h in range(DIFF_HEADS):
        q1, q2 = _half_masks(q_ref[0, :, _slab(h)])
        qs.extend([jnp.concatenate([q1, q2], axis=0)] if stacked else [q1, q2])

    def main_block(ref, j, h):
        if interleaved:
            start = pl.multiple_of(j * (tk * DIFF_HEADS), tk * DIFF_HEADS) + h
            return ref[0, pl.ds(start, tk, stride=DIFF_HEADS), :].astype(BF16)
        return ref[0, pl.ds(pl.multiple_of(j * tk, tk), tk), _slab(h)].astype(BF16)

    def front_mm(j):
        zs = []
        for idx in range(n_items):
            h = idx // nmap
            if j is not None:
                kblk = main_block(km_ref, j, h)
            elif diag_in_main:
                kblk = main_block(km_ref, qi, h)
            else:
                kblk = kd_ref[0, :, _slab(h)].astype(BF16)
            zs.append(lax.dot_general(kblk, qs[idx], NT_DIMS, preferred_element_type=F32))
        return zs

    def front_vpu(zs, j, sink):
        bias_ref = bd_ref if j is None else bm_ref
        for idx in range(n_items):
            s = zs[idx] + bias_ref[idx // nmap]
            sink(idx, s, jnp.max(s, axis=0, keepdims=True))
        if j is None:
            return jnp.zeros((), F32), jnp.asarray(qi, jnp.int32)
        return jnp.asarray(q0 - j * tk, F32), jnp.asarray(j, jnp.int32)

    def back_mm(source, diag_ref):
        return None

    def back_vpu(source, pre, scalars, diag_ref):
        dist0, j = scalars
        for idx in range(n_items):
            h = idx // nmap
            vblk = vd_ref[0, :, _slab(h)].astype(BF16) if diag_ref else main_block(vm_ref, j, h)
            off = slopes[h] * dist0
            m = m_ref[idx]
            m_new = jnp.maximum(m, source(idx, 1) - off)
            p = jnp.exp2(source(idx, 0) - (m_new + off))
            alpha = jnp.exp2(m - m_new)
            m_ref[idx] = m_new
            l_ref[idx] = alpha * l_ref[idx] + jnp.sum(p, axis=0, keepdims=True)
            pv = lax.dot_general(vblk, p.astype(BF16), TN_DIMS, preferred_element_type=F32)
            acc_ref[idx] = alpha * acc_ref[idx] + pv

    m_ref[...] = jnp.full(m_ref.shape, NEG, F32)
    l_ref[...] = jnp.zeros(l_ref.shape, F32)
    acc_ref[...] = jnp.zeros(acc_ref.shape, F32)
    _run_pipeline((front_mm, front_vpu, back_mm, back_vpu), (s_ref, mt_ref),
                  n_main=n_main(qi), diag_in_main=diag_in_main)

    for h in range(DIFF_HEADS):
        if stacked:
            o = acc_ref[h] * (1.0 / l_ref[h])
            o = o[:, 0:tq] - lam * o[:, tq:nv]
        else:
            o = (acc_ref[2 * h] * (1.0 / l_ref[2 * h])
                 - acc_ref[2 * h + 1] * (lam / l_ref[2 * h + 1]))
        ms = jnp.mean(o * o, axis=0, keepdims=True)
        o = o * lax.rsqrt(ms + EPS) * g_ref[...] * (1.0 - LAMBDA_INIT)
        o_ref[0, :, _slab(h)] = o.T.astype(o_ref.dtype)


def _softplus2(z):
    return jnp.maximum(z, jnp.log(1.0 + jnp.exp2(jnp.minimum(z, 126.0))) * LOG2E)


def _sb_kernel(q_ref, km_ref, vm_ref, um_ref, o_ref, zt_ref, tb_ref, t0_ref, carry_ref, acc_ref,
               *, tk):
    tq = q_ref.shape[1]
    qi = pl.program_id(1)
    slabs = range(SB_HEADS // 2)
    qs = [_half_masks(q_ref[0, :, _slab(s)]) for s in slabs]

    def block(ref, j, s):
        return ref[0, pl.ds(pl.multiple_of(j * tk, tk), tk), _slab(s)]

    def front_mm(j):
        zs = []
        for s in slabs:
            kblk = block(km_ref, qi if j is None else j, s)
            for qm in qs[s]:
                zs.append(lax.dot_general(kblk, qm, NT_DIMS, preferred_element_type=F32))
        return zs

    def front_vpu(zs, j, sink):
        earlier = None
        if j is None:
            earlier = (lax.broadcasted_iota(jnp.int32, (tk, tq), 0)
                       < lax.broadcasted_iota(jnp.int32, (tk, tq), 1))
        for idx, z in enumerate(zs):
            t = _softplus2(z)
            zt = z - t
            if earlier is not None:
                t = jnp.where(earlier, t, 0.0)
                zt = jnp.where(earlier, zt, NEG)
            tb = t.astype(BF16)
            sink(idx, zt, tb, tb[0:1, :].astype(F32))
        return (jnp.asarray(qi if j is None else j, jnp.int32),)

    def back_mm(source, diag_ref):
        return [jnp.dot(um_ref[...], source(idx, 1), preferred_element_type=F32)
                for idx in range(SB_HEADS)]

    def back_vpu(source, laters, scalars, diag_ref):
        (j,) = scalars
        for idx in range(SB_HEADS):
            vblk = block(vm_ref, j, idx // 2)
            carry = carry_ref[idx]
            later = laters[idx]
            w = jnp.exp2(source(idx, 0) - later - carry)
            pv = lax.dot_general(vblk, w.astype(BF16), TN_DIMS, preferred_element_type=F32)
            carry_ref[idx] = carry + later[0:1, :] + source(idx, 2)
            acc_ref[idx] += pv

    def finished():
        return jnp.min(carry_ref[...]) >= DEAD_LOG2

    carry_ref[...] = jnp.zeros(carry_ref.shape, F32)
    acc_ref[...] = jnp.zeros(acc_ref.shape, F32)
    _walk_until_finished((front_mm, front_vpu, back_mm, back_vpu), (zt_ref, tb_ref, t0_ref),
                         n_main=qi, finished=finished)
    row = lax.broadcasted_iota(jnp.int32, (SLAB, tq), 0)
    for s in slabs:
        o = jnp.where(row < HEAD_DIM, acc_ref[2 * s], acc_ref[2 * s + 1])
        o_ref[0, :, _slab(s)] = o.T.astype(o_ref.dtype)


def _sb_cache_kernel(q_ref, ktn_ref, vtn_ref, kt_hbm, vt_hbm, kn_ref, vn_ref, um_ref, ud_ref, o_ref,
                     kbuf_ref, vbuf_ref, sem_ref, carry_ref, acc_ref, *, tk):
    tq = q_ref.shape[1]
    nchunk = kt_hbm.shape[2] // tk
    b = pl.program_id(0)
    slabs = range(SB_HEADS // 2)
    qv = []
    for s in slabs:
        qa, qb = _half_masks(q_ref[0, :, _slab(s)])
        qv.append(jnp.concatenate([qa, qb], axis=0))

    def tile(k_of, v_of, u, earlier, new_keys):
        staged = []
        for s in slabs:
            if new_keys:
                z = lax.dot_general(qv[s], k_of(s), NT_DIMS, preferred_element_type=F32)
            else:
                z = jnp.dot(qv[s], k_of(s), preferred_element_type=F32)
            t = _softplus2(z)
            zt = z - t
            if earlier is not None:
                t = jnp.where(earlier, t, 0.0)
                zt = jnp.where(earlier, zt, NEG)
            staged.append((zt, t.astype(BF16)))
        tb_all = jnp.concatenate([tb for _, tb in staged], axis=0)
        later_all = jnp.dot(tb_all, u, preferred_element_type=F32)
        for s in slabs:
            zt, tb = staged[s]
            carry = carry_ref[s]
            later = later_all[s * 2 * tq:(s + 1) * 2 * tq]
            w = jnp.exp2(zt - later - carry).astype(BF16)
            if new_keys:
                pv = jnp.dot(w, v_of(s), preferred_element_type=F32)
            else:
                pv = lax.dot_general(w, v_of(s), NT_DIMS, preferred_element_type=F32)
            carry_ref[s] = carry + later[:, 0:1] + tb[:, 0:1].astype(F32)
            acc_ref[s] += pv

    def finished():
        return jnp.min(carry_ref[...]) >= DEAD_LOG2

    carry_ref[...] = jnp.zeros(carry_ref.shape, F32)
    acc_ref[...] = jnp.zeros(acc_ref.shape, F32)

    r = lax.broadcasted_iota(jnp.int32, (2 * tq, tq), 0) & (tq - 1)
    earlier = lax.broadcasted_iota(jnp.int32, (2 * tq, tq), 1) < r
    tile(lambda s: kn_ref[0, :, _slab(s)], lambda s: vn_ref[0, :, _slab(s)], ud_ref[...], earlier, True)
    tile(lambda s: ktn_ref[0, _slab(s), :].astype(BF16), lambda s: vtn_ref[0, _slab(s), :].astype(BF16),
         um_ref[...], None, False)

    def copies(j):
        lanes = pl.ds(pl.multiple_of(j * tk, tk), tk)
        return (pltpu.make_async_copy(kt_hbm.at[b, :, lanes], kbuf_ref, sem_ref.at[0]),
                pltpu.make_async_copy(vt_hbm.at[b, :, lanes], vbuf_ref, sem_ref.at[1]))

    def w_cond(carry):
        j, done = carry
        return jnp.logical_and(j >= 0, jnp.logical_not(done))

    def w_body(carry):
        j, _ = carry
        for c in copies(j):
            c.start()
        for c in copies(j):
            c.wait()
        tile(lambda s: kbuf_ref[_slab(s), :].astype(BF16), lambda s: vbuf_ref[_slab(s), :].astype(BF16),
             um_ref[...], None, False)
        return j - 1, finished()

    lax.while_loop(w_cond, w_body, (jnp.int32(nchunk - 2), finished()))

    lane = lax.broadcasted_iota(jnp.int32, (tq, SLAB), 1)
    for s in slabs:
        acc = acc_ref[s]
        o_ref[0, :, _slab(s)] = jnp.where(lane < HEAD_DIM, acc[0:tq], acc[tq:2 * tq]).astype(o_ref.dtype)


def _later_keys(n, keys_on_rows):
    r = np.arange(n)
    u = (r[None, :] > r[:, None]) if keys_on_rows else (r[:, None] > r[None, :])
    return jnp.asarray(u.astype(np.float32), dtype=BF16)


def _alibi_tables(tq, tk, td, stacked):
    nv = 2 * tq if stacked else tq
    qc = np.arange(nv) % tq
    slopes = np.asarray(ALIBI_SLOPES, np.float64)[:, None, None] * LOG2E
    main = -slopes * (qc[None, :] - np.arange(tk)[:, None])[None]
    r = np.arange(td)[:, None]
    diag = np.where((r // CHUNK) <= (qc[None, :] // CHUNK), -slopes * np.abs(qc[None, :] - r)[None], NEG)
    return jnp.asarray(main, F32), jnp.asarray(diag, F32)


def _prompt_attention(q, kb, vb, lam, subln_g, *, tq):
    b, t, _ = q.shape
    grid = (b, t // tq)
    cparams = pltpu.CompilerParams(dimension_semantics=("parallel", "arbitrary"),
                                   vmem_limit_bytes=VMEM_LIMIT)
    smem = pl.BlockSpec(memory_space=pltpu.SMEM)
    g_spec = pl.BlockSpec((SLAB, 1), lambda i, j: (0, 0))
    out_spec = pl.BlockSpec((1, tq, DIFF_WIDTH), lambda i, j: (i, j, 0))
    out_shape = jax.ShapeDtypeStruct((b, t, DIFF_WIDTH), BF16)

    def specs(half):
        return (pl.BlockSpec((1, tq, DIFF_WIDTH), lambda i, j: (i, j, half)),
                pl.BlockSpec((1, t, DIFF_WIDTH), lambda i, j: (i, 0, half)))

    q_spec, kv_spec = specs(0)
    tab_spec = pl.BlockSpec((DIFF_HEADS, tq, tq), lambda i, j: (0, 0, 0))
    od = pl.pallas_call(
        functools.partial(_diff_kernel, tk=tq, td=tq, n_main=lambda qi: qi,
                          q_base=lambda qi: qi * tq,
                          diag_in_main=True, stacked=False, interleaved=False),
        grid=grid, in_specs=[smem, q_spec, kv_spec, kv_spec, g_spec, tab_spec, tab_spec],
        out_specs=out_spec, out_shape=out_shape,
        scratch_shapes=[pltpu.VMEM((2, 2 * DIFF_HEADS, tq, tq), F32),
                        pltpu.VMEM((2, 2 * DIFF_HEADS, 1, tq), F32),
                        pltpu.VMEM((2 * DIFF_HEADS, 1, tq), F32),
                        pltpu.VMEM((2 * DIFF_HEADS, 1, tq), F32),
                        pltpu.VMEM((2 * DIFF_HEADS, SLAB, tq), F32)],
        compiler_params=cparams, name="diff_attention",
    )(lam, q, kb, vb, subln_g, *_alibi_tables(tq, tq, tq, False))

    q_spec, kv_spec = specs(1)
    osb = pl.pallas_call(
        functools.partial(_sb_kernel, tk=tq),
        grid=grid,
        in_specs=[q_spec, kv_spec, kv_spec, pl.BlockSpec((tq, tq), lambda i, j: (0, 0))],
        out_specs=out_spec, out_shape=out_shape,
        scratch_shapes=[pltpu.VMEM((1, SB_HEADS, tq, tq), F32),
                        pltpu.VMEM((1, SB_HEADS, tq, tq), BF16),
                        pltpu.VMEM((1, SB_HEADS, 1, tq), F32),
                        pltpu.VMEM((SB_HEADS, 1, tq), F32),
                        pltpu.VMEM((SB_HEADS, SLAB, tq), F32)],
        compiler_params=cparams, name="sb_attention",
    )(q, kb, vb, _later_keys(tq, True))
    return od, osb


def _sample_attention(q, kb, vb, cdk, cdv, cskt, csvt, lam, subln_g, *, tk):
    b, t, _ = q.shape
    past = cskt.shape[2]
    assert past % tk == 0 and past % CHUNK == 0 and t == CHUNK
    smem = pl.BlockSpec(memory_space=pltpu.SMEM)
    new = lambda half: pl.BlockSpec((1, t, DIFF_WIDTH), lambda i, j: (i, 0, half))
    out_spec = pl.BlockSpec((1, t, DIFF_WIDTH), lambda i, j: (i, 0, 0))
    out_shape = jax.ShapeDtypeStruct((b, t, DIFF_WIDTH), BF16)
    cparams = pltpu.CompilerParams(dimension_semantics=("parallel", "arbitrary"),
                                   vmem_limit_bytes=VMEM_LIMIT)

    rows = pl.BlockSpec((1, past * DIFF_HEADS, SLAB), lambda i, j: (i, 0, 0))
    od = pl.pallas_call(
        functools.partial(_diff_kernel, tk=tk, td=t, n_main=lambda qi: past // tk,
                          q_base=lambda qi: past,
                          diag_in_main=False, stacked=True, interleaved=True),
        grid=(b, 1),
        in_specs=[smem, new(0), rows, rows, new(0), new(0),
                  pl.BlockSpec((SLAB, 1), lambda i, j: (0, 0)),
                  pl.BlockSpec((DIFF_HEADS, tk, 2 * t), lambda i, j: (0, 0, 0)),
                  pl.BlockSpec((DIFF_HEADS, t, 2 * t), lambda i, j: (0, 0, 0))],
        out_specs=out_spec, out_shape=out_shape,
        scratch_shapes=[pltpu.VMEM((2, DIFF_HEADS, tk, 2 * t), F32),
                        pltpu.VMEM((2, DIFF_HEADS, 1, 2 * t), F32),
                        pltpu.VMEM((DIFF_HEADS, 1, 2 * t), F32),
                        pltpu.VMEM((DIFF_HEADS, 1, 2 * t), F32),
                        pltpu.VMEM((DIFF_HEADS, SLAB, 2 * t), F32)],
        compiler_params=cparams, name="diff_attention_cache",
    )(lam, q, cdk, cdv, kb, vb, subln_g, *_alibi_tables(t, tk, t, True))

    newest = pl.BlockSpec((1, SB_WIDTH, tk), lambda i, j: (i, 0, past // tk - 1))
    hbm = pl.BlockSpec(memory_space=pl.ANY)
    osb = pl.pallas_call(
        functools.partial(_sb_cache_kernel, tk=tk),
        grid=(b, 1),
        in_specs=[new(1), newest, newest, hbm, hbm, new(1), new(1),
                  pl.BlockSpec((tk, tk), lambda i, j: (0, 0)),
                  pl.BlockSpec((t, t), lambda i, j: (0, 0))],
        out_specs=out_spec, out_shape=out_shape,
        scratch_shapes=[pltpu.VMEM((SB_WIDTH, tk), F32), pltpu.VMEM((SB_WIDTH, tk), F32),
                        pltpu.SemaphoreType.DMA((2,)),
                        pltpu.VMEM((SB_HEADS // 2, 2 * t, 1), F32),
                        pltpu.VMEM((SB_HEADS // 2, 2 * t, SLAB), F32)],
        compiler_params=cparams, name="sb_attention_cache",
    )(q, cskt, csvt, cskt, csvt, kb, vb, _later_keys(tk, False), _later_keys(t, False))
    return od, osb


def _out_kernel(od_ref, osb_ref, x_ref, mod_ref, gpm_ref, gpf_ref, gqf_ref,
                wo_ref, wu_ref, wd_ref, y_ref):
    bb, tm, d = x_ref.shape
    if bb > 1:
        parts = [(slice(h * bb // OUT_PARTS, (h + 1) * bb // OUT_PARTS), slice(None))
                 for h in range(OUT_PARTS)]
    else:
        parts = [(slice(None), slice(h * tm // OUT_PARTS, (h + 1) * tm // OUT_PARTS))
                 for h in range(OUT_PARTS)]
    n = bb * tm // OUT_PARTS
    fc = 1024

    ys = []
    for bs, ts in parts:
        od = od_ref[bs, ts, :].reshape(n, DIFF_WIDTH)
        osb = osb_ref[bs, ts, :].reshape(n, SB_WIDTH)
        ys.append(jnp.dot(od, wo_ref[0:DIFF_WIDTH, :], preferred_element_type=F32)
                  + jnp.dot(osb, wo_ref[DIFF_WIDTH:MIX_WIDTH, :], preferred_element_type=F32))
    x1s, h2s = [], []
    for (bs, ts), y in zip(parts, ys):
        mod = mod_ref[bs]
        x = x_ref[bs, ts, :]
        x1 = x + mod[:, 2:3, :] * _rms(y.reshape(x.shape), gpm_ref[...])
        h2 = _rms(x1, gpf_ref[...]) * (1.0 + mod[:, 4:5, :]) + mod[:, 3:4, :]
        x1s.append(x1)
        h2s.append(h2.reshape(n, d).astype(BF16))
    accs = []
    for h2b in h2s:
        acc = jnp.zeros((n, d), F32)
        for c in range(D_FF // fc):
            f = jnp.dot(h2b, wu_ref[:, c * fc:(c + 1) * fc], preferred_element_type=F32)
            r = jnp.square(jnp.maximum(f, 0.0)).astype(BF16)
            acc = acc + jnp.dot(r, wd_ref[c * fc:(c + 1) * fc, :], preferred_element_type=F32)
        accs.append(acc)
    for (bs, ts), x1, acc in zip(parts, x1s, accs):
        y_ref[bs, ts, :] = x1 + mod_ref[bs][:, 5:6, :] * _rms(acc.reshape(x1.shape), gqf_ref[...])


def _out_ffn(od, osb, x, mods, mod_off, g_post_mix, g_pre_ffn, g_post_ffn,
             wo_bf, wu_bf, wd_bf, bb, tm):
    b, s, d = x.shape
    grid = (b // bb, s // tm)
    tok = lambda w: pl.BlockSpec((bb, tm, w), lambda i, t: (i, t, 0))
    const = lambda shape: pl.BlockSpec(shape, lambda i, t: (0,) * len(shape),
                                       pipeline_mode=pl.Buffered(1))
    return pl.pallas_call(
        _out_kernel,
        grid=grid,
        in_specs=[tok(DIFF_WIDTH), tok(SB_WIDTH), tok(d),
                  pl.BlockSpec((bb, 6, d), lambda i, t: (i + mod_off // bb, 0, 0)),
                  const((1, d)), const((1, d)), const((1, d)),
                  const((MIX_WIDTH, d)), const((d, D_FF)), const((D_FF, d))],
        out_specs=tok(d),
        out_shape=jax.ShapeDtypeStruct((b, s, d), F32),
        compiler_params=pltpu.CompilerParams(dimension_semantics=("parallel", "parallel"),
                                             vmem_limit_bytes=VMEM_LIMIT),
        name="out_ffn",
    )(od, osb, x, mods, g_post_mix, g_pre_ffn, g_post_ffn, wo_bf, wu_bf, wd_bf)


def kernel(x_prompt, x_sample, c_prompt, c_sample, cache_diff_k, cache_diff_v, cache_sb_k, cache_sb_v,
           w_ada, b_ada, g_pre_mix, g_post_mix, w_in, lambda_q1, lambda_k1, lambda_q2, lambda_k2,
           diff_subln_g, w_out, g_pre_ffn, g_post_ffn, w_up, w_down):
    bp, sp, d = x_prompt.shape
    bs, ss, _ = x_sample.shape
    past = cache_diff_k.shape[2]
    l = 0

    c_all = jnp.concatenate([c_prompt, c_sample], axis=0)
    mods, lam_tile = _modulation(c_all, w_ada[l], b_ada[l][None, :],
                                 lambda_q1[l][None, :], lambda_k1[l][None, :],
                                 lambda_q2[l][None, :], lambda_k2[l][None, :])
    mods = mods.reshape(bp + bs, 6, d)
    lam = lam_tile[0, 0:1]
    subln_g = diff_subln_g[l].reshape(SLAB, 1)

    w_in_bf = w_in[l].astype(BF16)
    w_out_bf = w_out[l].astype(BF16)
    w_up_bf = w_up[l].astype(BF16)
    w_down_bf = w_down[l].astype(BF16)
    g1 = g_pre_mix[l][None, :]
    g2 = g_post_mix[l][None, :]
    g3 = g_pre_ffn[l][None, :]
    g4 = g_post_ffn[l][None, :]
    diff_shape = lambda b, s: (1, b, s, DIFF_HEADS, 2 * HEAD_DIM)
    sb_shape = lambda b, s: (1, b, s, SB_HEADS, HEAD_DIM)
    sb_result = lambda a, b, s: jnp.transpose(a.reshape(b, SB_HEADS, HEAD_DIM, s),
                                              (0, 3, 1, 2)).reshape(sb_shape(b, s))

    q, kb, vb, kd, vd, ks, vs = _in_proj(x_prompt, mods, 0, g1, w_in_bf, bb=1, tm=512,
                                         sb_feature_major=True)
    od, osb = _prompt_attention(q, kb, vb, lam, subln_g, tq=256)
    y_prompt = _out_ffn(od, osb, x_prompt, mods, 0, g2, g3, g4,
                        w_out_bf, w_up_bf, w_down_bf, bb=1, tm=512)
    prompt_kv = (kd.reshape(diff_shape(bp, sp)), vd.reshape(diff_shape(bp, sp)),
                 sb_result(ks, bp, sp), sb_result(vs, bp, sp))

    q2, kb2, vb2, kd2, vd2, ks2, vs2 = _in_proj(x_sample, mods, bp, g1, w_in_bf, bb=8, tm=ss,
                                                sb_feature_major=False)
    cdk = cache_diff_k[l].reshape(bs, past * DIFF_HEADS, SLAB)
    cdv = cache_diff_v[l].reshape(bs, past * DIFF_HEADS, SLAB)
    cskt = jnp.transpose(cache_sb_k[l], (0, 2, 3, 1)).reshape(bs, SB_WIDTH, past)
    csvt = jnp.transpose(cache_sb_v[l], (0, 2, 3, 1)).reshape(bs, SB_WIDTH, past)
    od2, osb2 = _sample_attention(q2, kb2, vb2, cdk, cdv, cskt, csvt, lam, subln_g, tk=256)
    y_sample = _out_ffn(od2, osb2, x_sample, mods, bp, g2, g3, g4,
                        w_out_bf, w_up_bf, w_down_bf, bb=8, tm=ss)
    sample_kv = (kd2.reshape(diff_shape(bs, ss)), vd2.reshape(diff_shape(bs, ss)),
                 ks2.reshape(sb_shape(bs, ss)), vs2.reshape(sb_shape(bs, ss)))

    return (y_prompt, y_sample, *prompt_kv, *sample_kv)
```

```python
import functools
import math

import jax
import jax.numpy as jnp
import numpy as np
from jax import lax
from jax.experimental import pallas as pl
from jax.experimental.pallas import tpu as pltpu

D_MODEL = 1024
CHUNK = 64
HEAD_DIM = 64
DIFF_HEADS = 4
SB_HEADS = 8
SLAB = 128
DIFF_WIDTH = DIFF_HEADS * 2 * HEAD_DIM
SB_WIDTH = SB_HEADS * HEAD_DIM
MIX_WIDTH = DIFF_WIDTH + SB_WIDTH
IN_WIDTH = 3 * MIX_WIDTH
D_FF = 4 * D_MODEL
EPS = 1e-6
NEG = -1e30
LAMBDA_INIT = 0.8 - 0.6 * math.exp(-0.3 * 0)
ATTN_SCALE = HEAD_DIM ** -0.5
LOG2E = math.log2(math.e)
Q_SCALE = ATTN_SCALE * LOG2E
DEAD_LOG2 = 160.0
OUT_PARTS = 2
ALIBI_SLOPES = tuple(float(2.0 ** (-8.0 * (i + 1) / DIFF_HEADS)) for i in range(DIFF_HEADS))

V7X_VMEM_BYTES = 64 * 1024 * 1024
VMEM_LIMIT = V7X_VMEM_BYTES - 8 * 1024 * 1024
BF16 = jnp.bfloat16
F32 = jnp.float32

NT_DIMS = (((1,), (1,)), ((), ()))
TN_DIMS = (((0,), (0,)), ((), ()))


def _rms(x, g):
    ms = jnp.mean(x * x, axis=-1, keepdims=True)
    return x * lax.rsqrt(ms + EPS) * g


def _slab(i):
    return slice(i * SLAB, (i + 1) * SLAB)


def _mod_kernel(c_ref, w_ref, b_ref, lq1_ref, lk1_ref, lq2_ref, lk2_ref, m_ref, lam_ref):
    c = c_ref[...]
    s = c * jax.nn.sigmoid(c)
    m_ref[...] = jnp.dot(s, w_ref[...], preferred_element_type=F32,
                         precision=lax.Precision.HIGHEST) + b_ref[...]
    d1 = jnp.sum(lq1_ref[...] * lk1_ref[...], axis=-1, keepdims=True)
    d2 = jnp.sum(lq2_ref[...] * lk2_ref[...], axis=-1, keepdims=True)
    lam = jnp.exp(d1) - jnp.exp(d2) + LAMBDA_INIT
    lam_ref[...] = jnp.broadcast_to(lam, lam_ref.shape)


def _modulation(c_all, w_ada, b_ada, lq1, lk1, lq2, lk2):
    nb = c_all.shape[0]
    tn = 1024
    vec = pl.BlockSpec((1, HEAD_DIM), lambda j: (0, 0))
    return pl.pallas_call(
        _mod_kernel,
        grid=(6 * D_MODEL // tn,),
        in_specs=[pl.BlockSpec((nb, D_MODEL), lambda j: (0, 0)),
                  pl.BlockSpec((D_MODEL, tn), lambda j: (0, j)),
                  pl.BlockSpec((1, tn), lambda j: (0, j)),
                  vec, vec, vec, vec],
        out_specs=[pl.BlockSpec((nb, tn), lambda j: (0, j)),
                   pl.BlockSpec((8, 128), lambda j: (0, 0))],
        out_shape=[jax.ShapeDtypeStruct((nb, 6 * D_MODEL), F32),
                   jax.ShapeDtypeStruct((8, 128), F32)],
        compiler_params=pltpu.CompilerParams(dimension_semantics=("arbitrary",),
                                             vmem_limit_bytes=VMEM_LIMIT),
        name="modulation",
    )(c_all, w_ada, b_ada, lq1, lk1, lq2, lk2)


def _in_kernel(x_ref, mod_ref, g_ref, w_ref,
               q_ref, kb_ref, vb_ref, kd_ref, vd_ref, ks_ref, vs_ref, *, sb_feature_major):
    bb, tm, d = x_ref.shape
    x = x_ref[...]
    shift = mod_ref[:, 0:1, :]
    scale = mod_ref[:, 1:2, :]
    h = _rms(x, g_ref[...]) * (1.0 + scale) + shift
    hb = h.reshape(bb * tm, d).astype(BF16)

    def proj(c):
        u = jnp.dot(hb, w_ref[:, c * 512:(c + 1) * 512], preferred_element_type=F32)
        return u.reshape(bb, tm, 512)

    q_ref[:, :, 0:512] = (proj(0) * Q_SCALE).astype(BF16)
    q_ref[:, :, 512:1024] = (proj(3) * Q_SCALE).astype(BF16)
    for c, f_ref, b_ref in ((1, kd_ref, kb_ref), (2, vd_ref, vb_ref)):
        u = proj(c)
        b_ref[:, :, 0:512] = u.astype(BF16)
        for hd in range(DIFF_HEADS):
            f_ref[:, pl.ds(hd, tm, stride=DIFF_HEADS), :] = u[:, :, _slab(hd)]
    for c, f_ref, b_ref in ((4, ks_ref, kb_ref), (5, vs_ref, vb_ref)):
        u = proj(c)
        b_ref[:, :, 512:1024] = u.astype(BF16)
        if sb_feature_major:
            for i in range(bb):
                f_ref[i] = u[i].T
        else:
            f_ref[...] = u


def _in_proj(x, mods, mod_off, g, w_bf, bb, tm, sb_feature_major):
    b, s, d = x.shape
    grid = (b // bb, s // tm)
    tok = lambda w: pl.BlockSpec((bb, tm, w), lambda i, t: (i, t, 0))
    rows = pl.BlockSpec((bb, tm * DIFF_HEADS, SLAB), lambda i, t: (i, t, 0))
    const = lambda shape: pl.BlockSpec(shape, lambda i, t: (0,) * len(shape),
                                       pipeline_mode=pl.Buffered(1))
    outs = [jax.ShapeDtypeStruct((b, s, MIX_WIDTH), BF16)] * 3 + \
           [jax.ShapeDtypeStruct((b, s * DIFF_HEADS, SLAB), F32)] * 2 + \
           [jax.ShapeDtypeStruct((b, SB_WIDTH, s) if sb_feature_major else (b, s, SB_WIDTH), F32)] * 2
    sb_spec = (pl.BlockSpec((bb, SB_WIDTH, tm), lambda i, t: (i, 0, t)) if sb_feature_major
               else tok(SB_WIDTH))
    return pl.pallas_call(
        functools.partial(_in_kernel, sb_feature_major=sb_feature_major),
        grid=grid,
        in_specs=[tok(d),
                  pl.BlockSpec((bb, 6, d), lambda i, t: (i + mod_off // bb, 0, 0)),
                  const((1, d)),
                  const((d, IN_WIDTH))],
        out_specs=[tok(MIX_WIDTH)] * 3 + [rows] * 2 + [sb_spec] * 2,
        out_shape=outs,
        compiler_params=pltpu.CompilerParams(dimension_semantics=("parallel", "parallel"),
                                             vmem_limit_bytes=VMEM_LIMIT),
        name="in_proj",
    )(x, mods, g, w_bf)


def _half_masks(q):
    lane = lax.broadcasted_iota(jnp.int32, q.shape, 1)
    zero = jnp.zeros_like(q)
    return jnp.where(lane < HEAD_DIM, q, zero), jnp.where(lane >= HEAD_DIM, q, zero)


def _staging(refs, slot):
    def sink(idx, *vals):
        for r, v in zip(refs, vals):
            r[slot, idx] = v

    def source(idx, k):
        return refs[k][slot, idx]
    return sink, source


def _list_staging():
    store = {}

    def sink(idx, *vals):
        store[idx] = vals

    def source(idx, k):
        return store[idx][k]
    return sink, source


def _run_pipeline(stages, stage_refs, *, n_main, diag_in_main):
    front_mm, front_vpu, back_mm, back_vpu = stages

    def body(i, scalars):
        sink, _ = _staging(stage_refs, (i + 1) % 2)
        _, src = _staging(stage_refs, i % 2)
        j_next = n_main - 1 - i
        pre = back_mm(src, False)
        back_vpu(src, pre, scalars, False)
        return front_vpu(front_mm(j_next), j_next, sink)

    if diag_in_main:
        sink, _ = _staging(stage_refs, 0)
        scalars = front_vpu(front_mm(None), None, sink)
        first = 0
    else:
        sink, src = _list_staging()
        scalars = front_vpu(front_mm(None), None, sink)
        back_vpu(src, back_mm(src, True), scalars, True)
        sink, _ = _staging(stage_refs, 1)
        scalars = front_vpu(front_mm(n_main - 1), n_main - 1, sink)
        first = 1

    scalars = lax.fori_loop(first, n_main, body, scalars)
    _, src = _staging(stage_refs, n_main % 2)
    back_vpu(src, back_mm(src, False), scalars, False)


def _walk_until_finished(stages, stage_refs, *, n_main, finished):
    front_mm, front_vpu, back_mm, back_vpu = stages
    sink, src = _staging(stage_refs, 0)

    def tile(j):
        scalars = front_vpu(front_mm(j), j, sink)
        back_vpu(src, back_mm(src, False), scalars, False)

    def w_cond(carry):
        i, done = carry
        return jnp.logical_and(i < n_main, jnp.logical_not(done))

    def w_body(carry):
        i, _ = carry
        tile(n_main - 1 - i)
        return i + 1, finished()

    tile(None)
    lax.while_loop(w_cond, w_body, (jnp.int32(0), finished()))


def _diff_kernel(*refs, nq, **kw):
    if nq == 1:
        _diff_block(0, slice(None), refs, **kw)
        return
    tq = kw["tq"]

    def body(qi, carry):
        _diff_block(qi, pl.ds(pl.multiple_of(qi * tq, tq), tq), refs, **kw)
        return carry

    lax.fori_loop(0, nq, body, 0)


def _diff_block(qi, rows, refs, *, tq, tk, td, n_main, q_base, diag_in_main, stacked, interleaved):
    if diag_in_main:
        (lam_ref, q_ref, km_ref, vm_ref, g_ref, bm_ref, bd_ref, o_ref,
         s_ref, mt_ref, m_ref, l_ref, acc_ref) = refs
        kd_ref = vd_ref = None
    else:
        (lam_ref, q_ref, km_ref, vm_ref, kd_ref, vd_ref, g_ref, bm_ref, bd_ref, o_ref,
         s_ref, mt_ref, m_ref, l_ref, acc_ref) = refs
    nmap = 1 if stacked else 2
    nv = 2 * tq if stacked else tq
    n_items = nmap * DIFF_HEADS
    lam = lam_ref[0]
    q0 = q_base(qi)
    slopes = [s * LOG2E for s in ALIBI_SLOPES]
    qs = []
    for h in range(DIFF_HEADS):
        q1, q2 = _half_masks(q_ref[0, rows, _slab(h)])
        qs.extend([jnp.concatenate([q1, q2], axis=0)] if stacked else [q1, q2])

    def main_block(ref, j, h):
        if interleaved:
            start = pl.multiple_of(j * (tk * DIFF_HEADS), tk * DIFF_HEADS) + h
            return ref[0, pl.ds(start, tk, stride=DIFF_HEADS), :].astype(BF16)
        return ref[0, pl.ds(pl.multiple_of(j * tk, tk), tk), _slab(h)].astype(BF16)

    def front_mm(j):
        zs = []
        for idx in range(n_items):
            h = idx // nmap
            if j is not None:
                kblk = main_block(km_ref, j, h)
            elif diag_in_main:
                kblk = main_block(km_ref, qi, h)
            else:
                kblk = kd_ref[0, :, _slab(h)].astype(BF16)
            zs.append(lax.dot_general(kblk, qs[idx], NT_DIMS, preferred_element_type=F32))
        return zs

    def front_vpu(zs, j, sink):
        bias_ref = bd_ref if j is None else bm_ref
        for idx in range(n_items):
            s = zs[idx] + bias_ref[idx // nmap]
            sink(idx, s, jnp.max(s, axis=0, keepdims=True))
        if j is None:
            return jnp.zeros((), F32), jnp.asarray(qi, jnp.int32)
        return jnp.asarray(q0 - j * tk, F32), jnp.asarray(j, jnp.int32)

    def back_mm(source, diag_ref):
        return None

    def back_vpu(source, pre, scalars, diag_ref):
        dist0, j = scalars
        for idx in range(n_items):
            h = idx // nmap
            vblk = vd_ref[0, :, _slab(h)].astype(BF16) if diag_ref else main_block(vm_ref, j, h)
            off = slopes[h] * dist0
            m = m_ref[idx]
            m_new = jnp.maximum(m, source(idx, 1) - off)
            p = jnp.exp2(source(idx, 0) - (m_new + off))
            alpha = jnp.exp2(m - m_new)
            m_ref[idx] = m_new
            l_ref[idx] = alpha * l_ref[idx] + jnp.sum(p, axis=0, keepdims=True)
            pv = lax.dot_general(vblk, p.astype(BF16), TN_DIMS, preferred_element_type=F32)
            acc_ref[idx] = alpha * acc_ref[idx] + pv

    m_ref[...] = jnp.full(m_ref.shape, NEG, F32)
    l_ref[...] = jnp.zeros(l_ref.shape, F32)
    acc_ref[...] = jnp.zeros(acc_ref.shape, F32)
    _run_pipeline((front_mm, front_vpu, back_mm, back_vpu), (s_ref, mt_ref),
                  n_main=n_main(qi), diag_in_main=diag_in_main)

    for h in range(DIFF_HEADS):
        if stacked:
            o = acc_ref[h] * (1.0 / l_ref[h])
            o = o[:, 0:tq] - lam * o[:, tq:nv]
        else:
            o = (acc_ref[2 * h] * (1.0 / l_ref[2 * h])
                 - acc_ref[2 * h + 1] * (lam / l_ref[2 * h + 1]))
        ms = jnp.mean(o * o, axis=0, keepdims=True)
        o = o * lax.rsqrt(ms + EPS) * g_ref[...] * (1.0 - LAMBDA_INIT)
        o_ref[0, rows, _slab(h)] = o.T.astype(o_ref.dtype)


def _softplus2(z):
    return jnp.maximum(z, jnp.log(1.0 + jnp.exp2(jnp.minimum(z, 126.0))) * LOG2E)


def _sb_kernel(*refs, nq, tk):
    def body(qi, carry):
        _sb_block(qi, pl.ds(pl.multiple_of(qi * tk, tk), tk), *refs, tk=tk)
        return carry

    lax.fori_loop(0, nq, body, 0)


def _sb_block(qi, rows, q_ref, km_ref, vm_ref, um_ref, o_ref, zt_ref, tb_ref, t0_ref, carry_ref,
              acc_ref, *, tk):
    tq = tk
    slabs = range(SB_HEADS // 2)
    qs = [_half_masks(q_ref[0, rows, _slab(s)]) for s in slabs]

    def block(ref, j, s):
        return ref[0, pl.ds(pl.multiple_of(j * tk, tk), tk), _slab(s)]

    def front_mm(j):
        zs = []
        for s in slabs:
            kblk = block(km_ref, qi if j is None else j, s)
            for qm in qs[s]:
                zs.append(lax.dot_general(kblk, qm, NT_DIMS, preferred_element_type=F32))
        return zs

    def front_vpu(zs, j, sink):
        earlier = None
        if j is None:
            earlier = (lax.broadcasted_iota(jnp.int32, (tk, tq), 0)
                       < lax.broadcasted_iota(jnp.int32, (tk, tq), 1))
        for idx, z in enumerate(zs):
            t = _softplus2(z)
            zt = z - t
            if earlier is not None:
                t = jnp.where(earlier, t, 0.0)
                zt = jnp.where(earlier, zt, NEG)
            tb = t.astype(BF16)
            sink(idx, zt, tb, tb[0:1, :].astype(F32))
        return (jnp.asarray(qi if j is None else j, jnp.int32),)

    def back_mm(source, diag_ref):
        return [jnp.dot(um_ref[...], source(idx, 1), preferred_element_type=F32)
                for idx in range(SB_HEADS)]

    def back_vpu(source, laters, scalars, diag_ref):
        (j,) = scalars
        for idx in range(SB_HEADS):
            vblk = block(vm_ref, j, idx // 2)
            carry = carry_ref[idx]
            later = laters[idx]
            w = jnp.exp2(source(idx, 0) - later - carry)
            pv = lax.dot_general(vblk, w.astype(BF16), TN_DIMS, preferred_element_type=F32)
            carry_ref[idx] = carry + later[0:1, :] + source(idx, 2)
            acc_ref[idx] += pv

    def finished():
        return jnp.min(carry_ref[...]) >= DEAD_LOG2

    carry_ref[...] = jnp.zeros(carry_ref.shape, F32)
    acc_ref[...] = jnp.zeros(acc_ref.shape, F32)
    _walk_until_finished((front_mm, front_vpu, back_mm, back_vpu), (zt_ref, tb_ref, t0_ref),
                         n_main=qi, finished=finished)
    row = lax.broadcasted_iota(jnp.int32, (SLAB, tq), 0)
    for s in slabs:
        o = jnp.where(row < HEAD_DIM, acc_ref[2 * s], acc_ref[2 * s + 1])
        o_ref[0, rows, _slab(s)] = o.T.astype(o_ref.dtype)


def _sb_cache_kernel(q_ref, ktn_ref, vtn_ref, kt_hbm, vt_hbm, kn_ref, vn_ref, um_ref, ud_ref, o_ref,
                     kbuf_ref, vbuf_ref, sem_ref, carry_ref, acc_ref, *, tk):
    tq = q_ref.shape[1]
    nchunk = kt_hbm.shape[2] // tk
    b = pl.program_id(0)
    slabs = range(SB_HEADS // 2)
    qv = []
    for s in slabs:
        qa, qb = _half_masks(q_ref[0, :, _slab(s)])
        qv.append(jnp.concatenate([qa, qb], axis=0))

    def tile(k_of, v_of, u, earlier, new_keys):
        staged = []
        for s in slabs:
            if new_keys:
                z = lax.dot_general(qv[s], k_of(s), NT_DIMS, preferred_element_type=F32)
            else:
                z = jnp.dot(qv[s], k_of(s), preferred_element_type=F32)
            t = _softplus2(z)
            zt = z - t
            if earlier is not None:
                t = jnp.where(earlier, t, 0.0)
                zt = jnp.where(earlier, zt, NEG)
            staged.append((zt, t.astype(BF16)))
        tb_all = jnp.concatenate([tb for _, tb in staged], axis=0)
        later_all = jnp.dot(tb_all, u, preferred_element_type=F32)
        for s in slabs:
            zt, tb = staged[s]
            carry = carry_ref[s]
            later = later_all[s * 2 * tq:(s + 1) * 2 * tq]
            w = jnp.exp2(zt - later - carry).astype(BF16)
            if new_keys:
                pv = jnp.dot(w, v_of(s), preferred_element_type=F32)
            else:
                pv = lax.dot_general(w, v_of(s), NT_DIMS, preferred_element_type=F32)
            carry_ref[s] = carry + later[:, 0:1] + tb[:, 0:1].astype(F32)
            acc_ref[s] += pv

    def finished():
        return jnp.min(carry_ref[...]) >= DEAD_LOG2

    carry_ref[...] = jnp.zeros(carry_ref.shape, F32)
    acc_ref[...] = jnp.zeros(acc_ref.shape, F32)

    r = lax.broadcasted_iota(jnp.int32, (2 * tq, tq), 0) & (tq - 1)
    earlier = lax.broadcasted_iota(jnp.int32, (2 * tq, tq), 1) < r
    tile(lambda s: kn_ref[0, :, _slab(s)], lambda s: vn_ref[0, :, _slab(s)], ud_ref[...], earlier, True)
    tile(lambda s: ktn_ref[0, _slab(s), :].astype(BF16), lambda s: vtn_ref[0, _slab(s), :].astype(BF16),
         um_ref[...], None, False)

    def copies(j):
        lanes = pl.ds(pl.multiple_of(j * tk, tk), tk)
        return (pltpu.make_async_copy(kt_hbm.at[b, :, lanes], kbuf_ref, sem_ref.at[0]),
                pltpu.make_async_copy(vt_hbm.at[b, :, lanes], vbuf_ref, sem_ref.at[1]))

    def w_cond(carry):
        j, done = carry
        return jnp.logical_and(j >= 0, jnp.logical_not(done))

    def w_body(carry):
        j, _ = carry
        for c in copies(j):
            c.start()
        for c in copies(j):
            c.wait()
        tile(lambda s: kbuf_ref[_slab(s), :].astype(BF16), lambda s: vbuf_ref[_slab(s), :].astype(BF16),
             um_ref[...], None, False)
        return j - 1, finished()

    lax.while_loop(w_cond, w_body, (jnp.int32(nchunk - 2), finished()))

    lane = lax.broadcasted_iota(jnp.int32, (tq, SLAB), 1)
    for s in slabs:
        acc = acc_ref[s]
        o_ref[0, :, _slab(s)] = jnp.where(lane < HEAD_DIM, acc[0:tq], acc[tq:2 * tq]).astype(o_ref.dtype)


def _later_keys(n, keys_on_rows):
    r = np.arange(n)
    u = (r[None, :] > r[:, None]) if keys_on_rows else (r[:, None] > r[None, :])
    return jnp.asarray(u.astype(np.float32), dtype=BF16)


def _alibi_tables(tq, tk, td, stacked):
    nv = 2 * tq if stacked else tq
    qc = np.arange(nv) % tq
    slopes = np.asarray(ALIBI_SLOPES, np.float64)[:, None, None] * LOG2E
    main = -slopes * (qc[None, :] - np.arange(tk)[:, None])[None]
    r = np.arange(td)[:, None]
    diag = np.where((r // CHUNK) <= (qc[None, :] // CHUNK), -slopes * np.abs(qc[None, :] - r)[None], NEG)
    return jnp.asarray(main, F32), jnp.asarray(diag, F32)


def _prompt_attention(q, kb, vb, lam, subln_g, *, tq):
    b, t, _ = q.shape
    nq = t // tq
    grid = (b,)
    cparams = pltpu.CompilerParams(dimension_semantics=("parallel",),
                                   vmem_limit_bytes=VMEM_LIMIT)
    smem = pl.BlockSpec(memory_space=pltpu.SMEM)
    g_spec = pl.BlockSpec((SLAB, 1), lambda i: (0, 0))
    out_spec = pl.BlockSpec((1, t, DIFF_WIDTH), lambda i: (i, 0, 0))
    out_shape = jax.ShapeDtypeStruct((b, t, DIFF_WIDTH), BF16)
    seq = lambda half: pl.BlockSpec((1, t, DIFF_WIDTH), lambda i: (i, 0, half))

    q_spec = kv_spec = seq(0)
    tab_spec = pl.BlockSpec((DIFF_HEADS, tq, tq), lambda i: (0, 0, 0))
    od = pl.pallas_call(
        functools.partial(_diff_kernel, nq=nq, tq=tq, tk=tq, td=tq, n_main=lambda qi: qi,
                          q_base=lambda qi: qi * tq,
                          diag_in_main=True, stacked=False, interleaved=False),
        grid=grid, in_specs=[smem, q_spec, kv_spec, kv_spec, g_spec, tab_spec, tab_spec],
        out_specs=out_spec, out_shape=out_shape,
        scratch_shapes=[pltpu.VMEM((2, 2 * DIFF_HEADS, tq, tq), F32),
                        pltpu.VMEM((2, 2 * DIFF_HEADS, 1, tq), F32),
                        pltpu.VMEM((2 * DIFF_HEADS, 1, tq), F32),
                        pltpu.VMEM((2 * DIFF_HEADS, 1, tq), F32),
                        pltpu.VMEM((2 * DIFF_HEADS, SLAB, tq), F32)],
        compiler_params=cparams, name="diff_attention",
    )(lam, q, kb, vb, subln_g, *_alibi_tables(tq, tq, tq, False))

    q_spec = kv_spec = seq(1)
    osb = pl.pallas_call(
        functools.partial(_sb_kernel, nq=nq, tk=tq),
        grid=grid,
        in_specs=[q_spec, kv_spec, kv_spec, pl.BlockSpec((tq, tq), lambda i: (0, 0))],
        out_specs=out_spec, out_shape=out_shape,
        scratch_shapes=[pltpu.VMEM((1, SB_HEADS, tq, tq), F32),
                        pltpu.VMEM((1, SB_HEADS, tq, tq), BF16),
                        pltpu.VMEM((1, SB_HEADS, 1, tq), F32),
                        pltpu.VMEM((SB_HEADS, 1, tq), F32),
                        pltpu.VMEM((SB_HEADS, SLAB, tq), F32)],
        compiler_params=cparams, name="sb_attention",
    )(q, kb, vb, _later_keys(tq, True))
    return od, osb


def _sample_attention(q, kb, vb, cdk, cdv, cskt, csvt, lam, subln_g, *, tk):
    b, t, _ = q.shape
    past = cskt.shape[2]
    assert past % tk == 0 and past % CHUNK == 0 and t == CHUNK
    smem = pl.BlockSpec(memory_space=pltpu.SMEM)
    new = lambda half: pl.BlockSpec((1, t, DIFF_WIDTH), lambda i, j: (i, 0, half))
    out_spec = pl.BlockSpec((1, t, DIFF_WIDTH), lambda i, j: (i, 0, 0))
    out_shape = jax.ShapeDtypeStruct((b, t, DIFF_WIDTH), BF16)
    cparams = pltpu.CompilerParams(dimension_semantics=("parallel", "arbitrary"),
                                   vmem_limit_bytes=VMEM_LIMIT)

    rows = pl.BlockSpec((1, past * DIFF_HEADS, SLAB), lambda i, j: (i, 0, 0))
    od = pl.pallas_call(
        functools.partial(_diff_kernel, nq=1, tq=t, tk=tk, td=t, n_main=lambda qi: past // tk,
                          q_base=lambda qi: past,
                          diag_in_main=False, stacked=True, interleaved=True),
        grid=(b, 1),
        in_specs=[smem, new(0), rows, rows, new(0), new(0),
                  pl.BlockSpec((SLAB, 1), lambda i, j: (0, 0)),
                  pl.BlockSpec((DIFF_HEADS, tk, 2 * t), lambda i, j: (0, 0, 0)),
                  pl.BlockSpec((DIFF_HEADS, t, 2 * t), lambda i, j: (0, 0, 0))],
        out_specs=out_spec, out_shape=out_shape,
        scratch_shapes=[pltpu.VMEM((2, DIFF_HEADS, tk, 2 * t), F32),
                        pltpu.VMEM((2, DIFF_HEADS, 1, 2 * t), F32),
                        pltpu.VMEM((DIFF_HEADS, 1, 2 * t), F32),
                        pltpu.VMEM((DIFF_HEADS, 1, 2 * t), F32),
                        pltpu.VMEM((DIFF_HEADS, SLAB, 2 * t), F32)],
        compiler_params=cparams, name="diff_attention_cache",
    )(lam, q, cdk, cdv, kb, vb, subln_g, *_alibi_tables(t, tk, t, True))

    newest = pl.BlockSpec((1, SB_WIDTH, tk), lambda i, j: (i, 0, past // tk - 1))
    hbm = pl.BlockSpec(memory_space=pl.ANY)
    osb = pl.pallas_call(
        functools.partial(_sb_cache_kernel, tk=tk),
        grid=(b, 1),
        in_specs=[new(1), newest, newest, hbm, hbm, new(1), new(1),
                  pl.BlockSpec((tk, tk), lambda i, j: (0, 0)),
                  pl.BlockSpec((t, t), lambda i, j: (0, 0))],
        out_specs=out_spec, out_shape=out_shape,
        scratch_shapes=[pltpu.VMEM((SB_WIDTH, tk), F32), pltpu.VMEM((SB_WIDTH, tk), F32),
                        pltpu.SemaphoreType.DMA((2,)),
                        pltpu.VMEM((SB_HEADS // 2, 2 * t, 1), F32),
                        pltpu.VMEM((SB_HEADS // 2, 2 * t, SLAB), F32)],
        compiler_params=cparams, name="sb_attention_cache",
    )(q, cskt, csvt, cskt, csvt, kb, vb, _later_keys(tk, False), _later_keys(t, False))
    return od, osb


def _out_kernel(od_ref, osb_ref, x_ref, mod_ref, gpm_ref, gpf_ref, gqf_ref,
                wo_ref, wu_ref, wd_ref, y_ref):
    bb, tm, d = x_ref.shape
    if bb > 1:
        parts = [(slice(h * bb // OUT_PARTS, (h + 1) * bb // OUT_PARTS), slice(None))
                 for h in range(OUT_PARTS)]
    else:
        parts = [(slice(None), slice(h * tm // OUT_PARTS, (h + 1) * tm // OUT_PARTS))
                 for h in range(OUT_PARTS)]
    n = bb * tm // OUT_PARTS
    fc = 1024

    ys = []
    for bs, ts in parts:
        od = od_ref[bs, ts, :].reshape(n, DIFF_WIDTH)
        osb = osb_ref[bs, ts, :].reshape(n, SB_WIDTH)
        ys.append(jnp.dot(od, wo_ref[0:DIFF_WIDTH, :], preferred_element_type=F32)
                  + jnp.dot(osb, wo_ref[DIFF_WIDTH:MIX_WIDTH, :], preferred_element_type=F32))
    x1s, h2s = [], []
    for (bs, ts), y in zip(parts, ys):
        mod = mod_ref[bs]
        x = x_ref[bs, ts, :]
        x1 = x + mod[:, 2:3, :] * _rms(y.reshape(x.shape), gpm_ref[...])
        h2 = _rms(x1, gpf_ref[...]) * (1.0 + mod[:, 4:5, :]) + mod[:, 3:4, :]
        x1s.append(x1)
        h2s.append(h2.reshape(n, d).astype(BF16))
    accs = []
    for h2b in h2s:
        acc = jnp.zeros((n, d), F32)
        for c in range(D_FF // fc):
            f = jnp.dot(h2b, wu_ref[:, c * fc:(c + 1) * fc], preferred_element_type=F32)
            r = jnp.square(jnp.maximum(f, 0.0)).astype(BF16)
            acc = acc + jnp.dot(r, wd_ref[c * fc:(c + 1) * fc, :], preferred_element_type=F32)
        accs.append(acc)
    for (bs, ts), x1, acc in zip(parts, x1s, accs):
        y_ref[bs, ts, :] = x1 + mod_ref[bs][:, 5:6, :] * _rms(acc.reshape(x1.shape), gqf_ref[...])


def _out_ffn(od, osb, x, mods, mod_off, g_post_mix, g_pre_ffn, g_post_ffn,
             wo_bf, wu_bf, wd_bf, bb, tm):
    b, s, d = x.shape
    grid = (b // bb, s // tm)
    tok = lambda w: pl.BlockSpec((bb, tm, w), lambda i, t: (i, t, 0))
    const = lambda shape: pl.BlockSpec(shape, lambda i, t: (0,) * len(shape),
                                       pipeline_mode=pl.Buffered(1))
    return pl.pallas_call(
        _out_kernel,
        grid=grid,
        in_specs=[tok(DIFF_WIDTH), tok(SB_WIDTH), tok(d),
                  pl.BlockSpec((bb, 6, d), lambda i, t: (i + mod_off // bb, 0, 0)),
                  const((1, d)), const((1, d)), const((1, d)),
                  const((MIX_WIDTH, d)), const((d, D_FF)), const((D_FF, d))],
        out_specs=tok(d),
        out_shape=jax.ShapeDtypeStruct((b, s, d), F32),
        compiler_params=pltpu.CompilerParams(dimension_semantics=("parallel", "parallel"),
                                             vmem_limit_bytes=VMEM_LIMIT),
        name="out_ffn",
    )(od, osb, x, mods, g_post_mix, g_pre_ffn, g_post_ffn, wo_bf, wu_bf, wd_bf)


def kernel(x_prompt, x_sample, c_prompt, c_sample, cache_diff_k, cache_diff_v, cache_sb_k, cache_sb_v,
           w_ada, b_ada, g_pre_mix, g_post_mix, w_in, lambda_q1, lambda_k1, lambda_q2, lambda_k2,
           diff_subln_g, w_out, g_pre_ffn, g_post_ffn, w_up, w_down):
    bp, sp, d = x_prompt.shape
    bs, ss, _ = x_sample.shape
    past = cache_diff_k.shape[2]
    l = 0

    c_all = jnp.concatenate([c_prompt, c_sample], axis=0)
    mods, lam_tile = _modulation(c_all, w_ada[l], b_ada[l][None, :],
                                 lambda_q1[l][None, :], lambda_k1[l][None, :],
                                 lambda_q2[l][None, :], lambda_k2[l][None, :])
    mods = mods.reshape(bp + bs, 6, d)
    lam = lam_tile[0, 0:1]
    subln_g = diff_subln_g[l].reshape(SLAB, 1)

    w_in_bf = w_in[l].astype(BF16)
    w_out_bf = w_out[l].astype(BF16)
    w_up_bf = w_up[l].astype(BF16)
    w_down_bf = w_down[l].astype(BF16)
    g1 = g_pre_mix[l][None, :]
    g2 = g_post_mix[l][None, :]
    g3 = g_pre_ffn[l][None, :]
    g4 = g_post_ffn[l][None, :]
    diff_shape = lambda b, s: (1, b, s, DIFF_HEADS, 2 * HEAD_DIM)
    sb_shape = lambda b, s: (1, b, s, SB_HEADS, HEAD_DIM)
    sb_result = lambda a, b, s: jnp.transpose(a.reshape(b, SB_HEADS, HEAD_DIM, s),
                                              (0, 3, 1, 2)).reshape(sb_shape(b, s))

    q, kb, vb, kd, vd, ks, vs = _in_proj(x_prompt, mods, 0, g1, w_in_bf, bb=1, tm=512,
                                         sb_feature_major=True)
    od, osb = _prompt_attention(q, kb, vb, lam, subln_g, tq=256)
    y_prompt = _out_ffn(od, osb, x_prompt, mods, 0, g2, g3, g4,
                        w_out_bf, w_up_bf, w_down_bf, bb=1, tm=512)
    prompt_kv = (kd.reshape(diff_shape(bp, sp)), vd.reshape(diff_shape(bp, sp)),
                 sb_result(ks, bp, sp), sb_result(vs, bp, sp))

    q2, kb2, vb2, kd2, vd2, ks2, vs2 = _in_proj(x_sample, mods, bp, g1, w_in_bf, bb=8, tm=ss,
                                                sb_feature_major=False)
    cdk = cache_diff_k[l].reshape(bs, past * DIFF_HEADS, SLAB)
    cdv = cache_diff_v[l].reshape(bs, past * DIFF_HEADS, SLAB)
    cskt = jnp.transpose(cache_sb_k[l], (0, 2, 3, 1)).reshape(bs, SB_WIDTH, past)
    csvt = jnp.transpose(cache_sb_v[l], (0, 2, 3, 1)).reshape(bs, SB_WIDTH, past)
    od2, osb2 = _sample_attention(q2, kb2, vb2, cdk, cdv, cskt, csvt, lam, subln_g, tk=256)
    y_sample = _out_ffn(od2, osb2, x_sample, mods, bp, g2, g3, g4,
                        w_out_bf, w_up_bf, w_down_bf, bb=8, tm=ss)
    sample_kv = (kd2.reshape(diff_shape(bs, ss)), vd2.reshape(diff_shape(bs, ss)),
                 ks2.reshape(sb_shape(bs, ss)), vs2.reshape(sb_shape(bs, ss)))

    return (y_prompt, y_sample, *prompt_kv, *sample_kv)
```
